```python
import numpy as np
import jax
import jax.numpy as jnp
from jax import lax

D_MODEL = 2048
BATCH = 8
SEQ = 4096
DEPTH = 4
DEC_BATCH = 32
DEC_SEQ = 64
PAST_LEN = 1024

CHUNK = 64
MIX_W = D_MODEL
ATT_W = MIX_W // 2
SSM_W = MIX_W // 4
POOL_W = MIX_W - ATT_W - SSM_W
N_HEADS = 16
HEAD_DIM = ATT_W // N_HEADS
N_PREV_CHUNKS = 8
PREV_ROWS = N_PREV_CHUNKS * CHUNK
BAND_ROWS = PREV_ROWS + CHUNK
REL_CLIP = 256
SSM_GC = 16
SSM_GROUPS = SSM_W // SSM_GC
SSM_STATE = 64
DT_MIN = 1e-3
DT_MAX = 1e-1
POOL_WINDOWS = (2, 4, 8, 16)
POOL_GROUP_W = POOL_W // len(POOL_WINDOWS)
POOL_HIST = max(POOL_WINDOWS) - 1
IN_COLS = 4 * ATT_W + 2 * SSM_W + 2 * POOL_W
IN_SPLITS = (ATT_W, 2 * ATT_W, 3 * ATT_W, 4 * ATT_W, 4 * ATT_W + SSM_W,
             4 * ATT_W + 2 * SSM_W, 4 * ATT_W + 2 * SSM_W + POOL_W)
EPS = 1e-6
NEG_INF = -1e30

kernel_name = 'hymba_streaming_encoder_step'


def rmsnorm(x, g):
    xf = x.astype(jnp.float32)
    y = xf * lax.rsqrt(jnp.mean(xf * xf, axis=-1, keepdims=True) + EPS)
    return y.astype(x.dtype) * g


def chunk_band_attention(q, k, v, hist_k, hist_v, start_pos, rel_bias):
    B, L, H, Dh = q.shape
    W = hist_k.shape[1]
    n_chunks = -(-L // CHUNK)
    tail = n_chunks * CHUNK - L
    lead = PREV_ROWS - W
    end_pos = start_pos + L

    def extend(hist, new):
        return jnp.concatenate([jnp.zeros((B, lead, H, Dh), new.dtype), hist.astype(new.dtype), new,
                                jnp.zeros((B, tail, H, Dh), new.dtype)], axis=1)

    k_ext = extend(hist_k, k)
    v_ext = extend(hist_v, v)
    q_pad = jnp.pad(q, ((0, 0), (0, tail), (0, 0), (0, 0)))
    scale = HEAD_DIM ** -0.5

    def one_chunk(n):
        q0 = n * CHUNK
        qc = lax.dynamic_slice_in_dim(q_pad, q0, CHUNK, axis=1)
        kc = lax.dynamic_slice_in_dim(k_ext, q0, BAND_ROWS, axis=1)
        vc = lax.dynamic_slice_in_dim(v_ext, q0, BAND_ROWS, axis=1)
        q_pos = start_pos + q0 + jnp.arange(CHUNK)
        k_pos = start_pos - PREV_ROWS + q0 + jnp.arange(BAND_ROWS)
        q_ch = q_pos // CHUNK
        k_ch = k_pos // CHUNK
        mask = ((k_pos >= 0)[None, :] & (k_pos < end_pos)[None, :]
                & (k_ch[None, :] <= q_ch[:, None]) & (k_ch[None, :] >= q_ch[:, None] - N_PREV_CHUNKS))
        rel = jnp.clip(q_pos[:, None] - k_pos[None, :], -REL_CLIP, REL_CLIP) + REL_CLIP
        bias = rel_bias[:, rel].astype(jnp.float32)
        s = jnp.einsum('bqhd,bkhd->bhqk', qc, kc, preferred_element_type=jnp.float32) * scale + bias
        s = jnp.where(mask[None, None], s, NEG_INF)
        p = jax.nn.softmax(s, axis=-1).astype(vc.dtype)
        return jnp.einsum('bhqk,bkhd->bqhd', p, vc)

    out = lax.map(one_chunk, jnp.arange(n_chunks))
    out = jnp.moveaxis(out, 0, 1).reshape(B, n_chunks * CHUNK, H, Dh)
    return out[:, :L]


def _complex_affine_combine(e1, e2):
    a1r, a1i, b1r, b1i = e1
    a2r, a2i, b2r, b2i = e2
    ar = a2r * a1r - a2i * a1i
    ai = a2r * a1i + a2i * a1r
    br = a2r * b1r - a2i * b1i + b2r
    bi = a2r * b1i + a2i * b1r + b2i
    return ar, ai, br, bi


def s5_branch(u, h0_re, h0_im, a_re, a_im, log_dt, b_re, b_im, c_re, c_im, d_skip, w_glu, b_glu):
    f32 = jnp.float32
    Bsz, L, _ = u.shape
    uf = u.astype(f32).reshape(Bsz, L, SSM_GROUPS, SSM_GC)
    dt = jnp.exp(log_dt.astype(f32))[:, None]
    ar, ai = a_re.astype(f32), a_im.astype(f32)
    mag = jnp.exp(ar * dt)
    ang = ai * dt
    abar_r, abar_i = mag * jnp.cos(ang), mag * jnp.sin(ang)
    den = ar * ar + ai * ai
    nr, ni = abar_r - 1.0, abar_i
    coef_r = (nr * ar + ni * ai) / den
    coef_i = (ni * ar - nr * ai) / den
    br, bi = b_re.astype(f32), b_im.astype(f32)
    bbar_r = coef_r[..., None] * br - coef_i[..., None] * bi
    bbar_i = coef_r[..., None] * bi + coef_i[..., None] * br
    bu_r = jnp.einsum('blgc,gpc->blgp', uf, bbar_r)
    bu_i = jnp.einsum('blgc,gpc->blgp', uf, bbar_i)
    h0r, h0i = h0_re.astype(f32), h0_im.astype(f32)
    bu_r = bu_r.at[:, 0].add(abar_r * h0r - abar_i * h0i)
    bu_i = bu_i.at[:, 0].add(abar_r * h0i + abar_i * h0r)
    a_seq_r = jnp.broadcast_to(abar_r, (1, L, SSM_GROUPS, SSM_STATE))
    a_seq_i = jnp.broadcast_to(abar_i, (1, L, SSM_GROUPS, SSM_STATE))
    _, _, hr, hi = lax.associative_scan(_complex_affine_combine, (a_seq_r, a_seq_i, bu_r, bu_i), axis=1)
    y = (jnp.einsum('blgp,gcp->blgc', hr, c_re.astype(f32))
         - jnp.einsum('blgp,gcp->blgc', hi, c_im.astype(f32))
         + d_skip.astype(f32) * uf).reshape(Bsz, L, SSM_W)
    g = jax.nn.gelu(y)
    gl = g @ w_glu.astype(f32) + b_glu.astype(f32)
    out = gl[..., :SSM_W] * jax.nn.sigmoid(gl[..., SSM_W:])
    return out.astype(u.dtype), hr[:, -1], hi[:, -1]


def pool_mix(ext, first_pos, w_pool, pool_scale):
    f32 = jnp.float32
    B, E, _ = ext.shape
    L = E - POOL_HIST
    ef = ext.astype(f32)
    cs = jnp.concatenate([jnp.zeros((B, 1, POOL_W), f32), jnp.cumsum(ef, axis=1)], axis=1)
    pos = first_pos + POOL_HIST + jnp.arange(L)
    tok = ef[:, POOL_HIST:]
    outs = []
    for g, w in enumerate(POOL_WINDOWS):
        lo, hi = g * POOL_GROUP_W, (g + 1) * POOL_GROUP_W
        wsum = cs[:, POOL_HIST + 1:POOL_HIST + 1 + L, lo:hi] - cs[:, POOL_HIST + 1 - w:POOL_HIST + 1 - w + L, lo:hi]
        cnt = jnp.minimum(w, pos + 1).astype(f32)
        diff = wsum / cnt[None, :, None] - tok[..., lo:hi]
        outs.append(jnp.einsum('blc,cd->bld', diff, w_pool[g].astype(f32)))
    return (jnp.concatenate(outs, axis=-1) * pool_scale.astype(f32)).astype(ext.dtype)


def trunk_layer(x, c, hist_k, hist_v, h0_re, h0_im, pool_hist, start_pos,
                norm_g, w_ada, b_ada, w_in, rel_bias, a_re, a_im, log_dt, b_re, b_im, c_re, c_im,
                d_skip, w_glu, b_glu, w_pool, pool_scale, branch_g, w_out):
    B, L, _ = x.shape
    mod = jax.nn.silu(c) @ w_ada + b_ada
    shift, scale, gate = jnp.split(mod, 3, axis=-1)
    h = rmsnorm(x, norm_g) * (1.0 + scale[:, None]) + shift[:, None]
    proj = h @ w_in
    q, k, v, z_att, u_ssm, z_ssm, u_pool, z_pool = jnp.split(proj, IN_SPLITS, axis=-1)
    hd = (B, L, N_HEADS, HEAD_DIM)
    q, k, v = q.reshape(hd), k.reshape(hd), v.reshape(hd)
    y_att = chunk_band_attention(q, k, v, hist_k, hist_v, start_pos, rel_bias).reshape(B, L, ATT_W)
    y_ssm, hT_re, hT_im = s5_branch(u_ssm, h0_re, h0_im, a_re, a_im, log_dt, b_re, b_im,
                                    c_re, c_im, d_skip, w_glu, b_glu)
    pool_ext = jnp.concatenate([pool_hist.astype(u_pool.dtype), u_pool], axis=1)
    y_pool = pool_mix(pool_ext, start_pos - POOL_HIST, w_pool, pool_scale)
    g_att, g_ssm, g_pool = jnp.split(branch_g, (ATT_W, ATT_W + SSM_W))
    y = jnp.concatenate([rmsnorm(y_att, g_att) * jax.nn.silu(z_att),
                         rmsnorm(y_ssm, g_ssm) * jax.nn.silu(z_ssm),
                         rmsnorm(y_pool, g_pool) * jax.nn.silu(z_pool)], axis=-1)
    x = x + gate[:, None] * (y @ w_out)
    return x, k, v, hT_re, hT_im, pool_ext[:, -POOL_HIST:]


def setup_inputs(seed: int = 0) -> dict:
    key = jax.random.key(seed)
    ks = iter(jax.random.split(key, 40))
    f32 = jnp.float32

    def nrm(shape, s):
        return s * jax.random.normal(next(ks), shape, f32)

    att_hist = min(PREV_ROWS, PAST_LEN)
    n_idx = jnp.arange(SSM_STATE, dtype=f32)
    return {
        'x_prompt': nrm((BATCH, SEQ, D_MODEL), 1.0),
        'x_sample': nrm((DEC_BATCH, DEC_SEQ, D_MODEL), 1.0),
        'c_prompt': nrm((BATCH, D_MODEL), 1.0),
        'c_sample': nrm((DEC_BATCH, D_MODEL), 1.0),
        'cache_k': nrm((DEPTH, DEC_BATCH, att_hist, N_HEADS, HEAD_DIM), 1.0),
        'cache_v': nrm((DEPTH, DEC_BATCH, att_hist, N_HEADS, HEAD_DIM), 1.0),
        'state_ssm_re': nrm((DEPTH, DEC_BATCH, SSM_GROUPS, SSM_STATE), 0.1),
        'state_ssm_im': nrm((DEPTH, DEC_BATCH, SSM_GROUPS, SSM_STATE), 0.1),
        'state_pool': nrm((DEPTH, DEC_BATCH, POOL_HIST, POOL_W), 1.0),
        'norm_g': 1.0 + nrm((DEPTH, D_MODEL), 0.05),
        'w_ada': nrm((DEPTH, D_MODEL, 3 * D_MODEL), 0.5 * D_MODEL ** -0.5),
        'b_ada': nrm((DEPTH, 3 * D_MODEL), 0.02),
        'w_in': nrm((DEPTH, D_MODEL, IN_COLS), D_MODEL ** -0.5),
        'rel_bias': nrm((DEPTH, N_HEADS, 2 * REL_CLIP + 1), 0.5),
        'ssm_a_re': -0.5 + nrm((DEPTH, SSM_GROUPS, SSM_STATE), 0.01),
        'ssm_a_im': jnp.pi * n_idx + nrm((DEPTH, SSM_GROUPS, SSM_STATE), 0.01),
        'ssm_log_dt': jax.random.uniform(next(ks), (DEPTH, SSM_GROUPS), f32,
                                         minval=float(np.log(DT_MIN)), maxval=float(np.log(DT_MAX))),
        'ssm_b_re': nrm((DEPTH, SSM_GROUPS, SSM_STATE, SSM_GC), SSM_GC ** -0.5),
        'ssm_b_im': nrm((DEPTH, SSM_GROUPS, SSM_STATE, SSM_GC), SSM_GC ** -0.5),
        'ssm_c_re': nrm((DEPTH, SSM_GROUPS, SSM_GC, SSM_STATE), SSM_STATE ** -0.5),
        'ssm_c_im': nrm((DEPTH, SSM_GROUPS, SSM_GC, SSM_STATE), SSM_STATE ** -0.5),
        'ssm_d': nrm((DEPTH, SSM_GROUPS, SSM_GC), 0.5),
        'w_glu': nrm((DEPTH, SSM_W, 2 * SSM_W), SSM_W ** -0.5),
        'b_glu': nrm((DEPTH, 2 * SSM_W), 0.02),
        'w_pool': nrm((DEPTH, len(POOL_WINDOWS), POOL_GROUP_W, POOL_GROUP_W), POOL_GROUP_W ** -0.5),
        'pool_scale': 1.0 + nrm((DEPTH, POOL_W), 0.1),
        'branch_norm_g': 1.0 + nrm((DEPTH, MIX_W), 0.05),
        'w_out': nrm((DEPTH, MIX_W, D_MODEL), MIX_W ** -0.5),
        'final_norm_g': 1.0 + nrm((D_MODEL,), 0.05),
    }


def reference(x_prompt, x_sample, c_prompt, c_sample, cache_k, cache_v, state_ssm_re, state_ssm_im,
              state_pool, norm_g, w_ada, b_ada, w_in, rel_bias, ssm_a_re, ssm_a_im, ssm_log_dt,
              ssm_b_re, ssm_b_im, ssm_c_re, ssm_c_im, ssm_d, w_glu, b_glu, w_pool, pool_scale,
              branch_norm_g, w_out, final_norm_g):
    Bp, Lp, _ = x_prompt.shape
    keep = min(PREV_ROWS, Lp)
    hk0 = jnp.zeros((Bp, 0, N_HEADS, HEAD_DIM), x_prompt.dtype)
    h00 = jnp.zeros((Bp, SSM_GROUPS, SSM_STATE), jnp.float32)
    ph0 = jnp.zeros((Bp, POOL_HIST, POOL_W), x_prompt.dtype)
    xp, xs = x_prompt, x_sample
    kp_l, vp_l, srp_l, sip_l, pp_l = [], [], [], [], []
    ks_l, vs_l, srs_l, sis_l, ps_l = [], [], [], [], []
    for l in range(DEPTH):
        lw = (norm_g[l], w_ada[l], b_ada[l], w_in[l], rel_bias[l], ssm_a_re[l], ssm_a_im[l],
              ssm_log_dt[l], ssm_b_re[l], ssm_b_im[l], ssm_c_re[l], ssm_c_im[l], ssm_d[l],
              w_glu[l], b_glu[l], w_pool[l], pool_scale[l], branch_norm_g[l], w_out[l])
        xp, k_p, v_p, sr_p, si_p, pool_p = trunk_layer(xp, c_prompt, hk0, hk0, h00, h00, ph0, 0, *lw)
        kp_l.append(k_p[:, Lp - keep:])
        vp_l.append(v_p[:, Lp - keep:])
        srp_l.append(sr_p)
        sip_l.append(si_p)
        pp_l.append(pool_p)
        xs, k_s, v_s, sr_s, si_s, pool_s = trunk_layer(xs, c_sample, cache_k[l], cache_v[l], state_ssm_re[l],
                                                       state_ssm_im[l], state_pool[l], PAST_LEN, *lw)
        ks_l.append(k_s)
        vs_l.append(v_s)
        srs_l.append(sr_s)
        sis_l.append(si_s)
        ps_l.append(pool_s)
    y_prompt = rmsnorm(xp, final_norm_g)
    y_sample = rmsnorm(xs, final_norm_g)
    return (y_prompt, y_sample,
            jnp.stack(kp_l), jnp.stack(vp_l), jnp.stack(srp_l), jnp.stack(sip_l), jnp.stack(pp_l),
            jnp.stack(ks_l), jnp.stack(vs_l), jnp.stack(srs_l), jnp.stack(sis_l), jnp.stack(ps_l))
```

```python
import functools

import jax
import jax.numpy as jnp
from jax import lax
from jax.experimental import pallas as pl
from jax.experimental.pallas import tpu as pltpu

F32 = jnp.float32
BF16 = jnp.bfloat16

D_MODEL = 2048
ATT_W = 1024
SSM_W = 512
POOL_W = 512
N_HEADS = 16
HEAD_DIM = 64
CHUNK = 64
PREV_ROWS = 512
BAND_CHUNKS = PREV_ROWS // CHUNK + 1
REL_CLIP = 256
PAST_LEN = 1024
SSM_GROUPS = 32
SSM_STATE = 64
SSM_GC = 16
N_STATES = SSM_GROUPS * SSM_STATE
POOL_WINDOWS = (2, 4, 8, 16)
POOL_GROUP_W = 128
POOL_HALO = 16
POOL_HIST = 15
IN_COLS = 6144
COL_Q, COL_K, COL_V, COL_ZATT = 0, 1024, 2048, 3072
COL_USSM, COL_ZSSM, COL_UPOOL, COL_ZPOOL = 4096, 4608, 5120, 5632
EPS = 1e-6
NEG_INF = -1e30

LANE = 128
SSM_LANE_TILES = SSM_W // LANE
SSM_TILE_STATES = N_STATES // SSM_LANE_TILES
VMEM_LIMIT = 56 * 1024 * 1024


def _params(sem):
    return pltpu.CompilerParams(dimension_semantics=sem, vmem_limit_bytes=VMEM_LIMIT)


def _ada_kernel(c_ref, w_ref, b_ref, o_ref):
    a = jax.nn.silu(c_ref[...]).astype(BF16)
    o_ref[0] = jnp.dot(a, w_ref[0].astype(BF16), preferred_element_type=F32) + b_ref[0]


def _ada_call(c_all, w_ada, b_ada):
    depth = w_ada.shape[0]
    rows = c_all.shape[0]
    tn = 768
    return pl.pallas_call(
        _ada_kernel,
        grid=(depth, 3 * D_MODEL // tn),
        in_specs=[
            pl.BlockSpec((rows, D_MODEL), lambda l, j: (0, 0)),
            pl.BlockSpec((1, D_MODEL, tn), lambda l, j: (l, 0, j)),
            pl.BlockSpec((1, 1, tn), lambda l, j: (l, 0, j)),
        ],
        out_specs=pl.BlockSpec((1, rows, tn), lambda l, j: (l, 0, j)),
        out_shape=jax.ShapeDtypeStruct((depth, rows, 3 * D_MODEL), F32),
        compiler_params=_params(("arbitrary", "arbitrary")),
        name="ada_mod",
    )(c_all, w_ada, b_ada.reshape(depth, 1, 3 * D_MODEL))


def _inproj_kernel(x_ref, sh_ref, sc_ref, g_ref, w_ref, proj_ref, utm_ref, h_scr, *, bb, tl, tn):
    j = pl.program_id(2)

    @pl.when(j == 0)
    def _():
        x = x_ref[...]
        ms = jnp.mean(x * x, axis=-1, keepdims=True)
        y = x * lax.rsqrt(ms + EPS) * g_ref[...]
        h = y * (1.0 + sc_ref[...]) + sh_ref[...]
        h_scr[...] = h.reshape(bb * tl, D_MODEL).astype(BF16)

    res = jnp.dot(h_scr[...], w_ref[...], preferred_element_type=F32)
    proj_ref[...] = res.reshape(bb, tl, tn)

    @pl.when(j == COL_USSM // tn)
    def _():
        c0 = COL_USSM % tn
        for b in range(bb):
            utm_ref[:, b * SSM_W:(b + 1) * SSM_W] = res[b * tl:(b + 1) * tl, c0:c0 + SSM_W]


def _inproj_call(x, shift, scale, g, w_bf16, *, bb, tl):
    B, L, _ = x.shape
    tn = 1024
    kern = functools.partial(_inproj_kernel, bb=bb, tl=tl, tn=tn)
    return pl.pallas_call(
        kern,
        grid=(B // bb, L // tl, IN_COLS // tn),
        in_specs=[
            pl.BlockSpec((bb, tl, D_MODEL), lambda b, i, j: (b, i, 0)),
            pl.BlockSpec((bb, 1, D_MODEL), lambda b, i, j: (b, 0, 0)),
            pl.BlockSpec((bb, 1, D_MODEL), lambda b, i, j: (b, 0, 0)),
            pl.BlockSpec((1, D_MODEL), lambda b, i, j: (0, 0)),
            pl.BlockSpec((D_MODEL, tn), lambda b, i, j: (0, j)),
        ],
        out_specs=[
            pl.BlockSpec((bb, tl, tn), lambda b, i, j: (b, i, j)),
            pl.BlockSpec((tl, bb * SSM_W), lambda b, i, j: (i, b)),
        ],
        out_shape=[
            jax.ShapeDtypeStruct((B, L, IN_COLS), F32),
            jax.ShapeDtypeStruct((L, B * SSM_W), F32),
        ],
        scratch_shapes=[pltpu.VMEM((bb * tl, D_MODEL), BF16)],
        compiler_params=_params(("arbitrary", "arbitrary", "arbitrary")),
        name="in_proj",
    )(x, shift, scale, g, w_bf16)


def _attn_kernel(q_ref, kp_ref, kc_ref, vp_ref, vc_ref, bias_ref, o_ref, kw, vw, *, tq, sq, hist):
    i = pl.program_id(2)
    kw[0:PREV_ROWS, :] = kp_ref[0].astype(BF16)
    kw[PREV_ROWS:PREV_ROWS + tq, :] = kc_ref[0].astype(BF16)
    vw[0:PREV_ROWS, :] = vp_ref[0].astype(BF16)
    vw[PREV_ROWS:PREV_ROWS + tq, :] = vc_ref[0].astype(BF16)
    wl = PREV_ROWS + sq
    col = lax.broadcasted_iota(jnp.int32, (sq, wl), 1)
    for s in range(tq // sq):
        valid = col >= PREV_ROWS - hist - i * tq - s * sq
        q = (q_ref[0, s * sq:(s + 1) * sq, :] * (HEAD_DIM ** -0.5)).astype(BF16)
        kwin = kw[s * sq:s * sq + wl, :]
        vwin = vw[s * sq:s * sq + wl, :]
        outs = []
        for hh in range(LANE // HEAD_DIM):
            hs = slice(hh * HEAD_DIM, (hh + 1) * HEAD_DIM)
            sc = lax.dot_general(q[:, hs], kwin[:, hs], (((1,), (1,)), ((), ())),
                                 preferred_element_type=F32)
            sc = jnp.where(valid, sc + bias_ref[hh], NEG_INF)
            m = jnp.max(sc, axis=-1, keepdims=True)
            e = jnp.exp(sc - m)
            den = jnp.sum(e, axis=-1, keepdims=True)
            o = jnp.dot(e.astype(BF16), vwin[:, hs], preferred_element_type=F32)
            outs.append(o / den)
        o_ref[0, s * sq:(s + 1) * sq, :] = jnp.concatenate(outs, axis=-1)


def _attn_call(proj, k_prev, v_prev, bias, *, tq, sq, hist):
    B, L, _ = proj.shape
    hp_blocks = ATT_W // LANE
    qb, kb, vb = COL_Q // LANE, COL_K // LANE, COL_V // LANE
    if k_prev is None:
        assert tq == PREV_ROWS
        k_prev, v_prev = proj, proj
        kp_map = lambda b, h, i: (b, jnp.maximum(i - 1, 0), kb + h)
        vp_map = lambda b, h, i: (b, jnp.maximum(i - 1, 0), vb + h)
    else:
        kp_map = lambda b, h, i: (b, 0, h)
        vp_map = lambda b, h, i: (b, 0, h)
    kern = functools.partial(_attn_kernel, tq=tq, sq=sq, hist=hist)
    return pl.pallas_call(
        kern,
        grid=(B, hp_blocks, L // tq),
        in_specs=[
            pl.BlockSpec((1, tq, LANE), lambda b, h, i: (b, i, qb + h)),
            pl.BlockSpec((1, PREV_ROWS, LANE), kp_map),
            pl.BlockSpec((1, tq, LANE), lambda b, h, i: (b, i, kb + h)),
            pl.BlockSpec((1, PREV_ROWS, LANE), vp_map),
            pl.BlockSpec((1, tq, LANE), lambda b, h, i: (b, i, vb + h)),
            pl.BlockSpec((LANE // HEAD_DIM, sq, PREV_ROWS + sq), lambda b, h, i: (h, 0, 0)),
        ],
        out_specs=pl.BlockSpec((1, tq, LANE), lambda b, h, i: (b, i, h)),
        out_shape=jax.ShapeDtypeStruct((B, L, ATT_W), F32),
        scratch_shapes=[pltpu.VMEM((PREV_ROWS + tq, LANE), BF16),
                        pltpu.VMEM((PREV_ROWS + tq, LANE), BF16)],
        compiler_params=_params(("arbitrary", "arbitrary", "arbitrary")),
        name="band_attn",
    )(proj, k_prev, proj, v_prev, proj, bias)


def _attn_bias(rel_bias_l, sq):
    i = jnp.arange(sq)[:, None]
    r = jnp.arange(PREV_ROWS + sq)[None, :]
    rel = jnp.clip(PREV_ROWS + i - r, -REL_CLIP, REL_CLIP) + REL_CLIP
    lo = (i // CHUNK) * CHUNK
    band = (r >= lo) & (r < lo + BAND_CHUNKS * CHUNK)
    return jnp.where(band[None], rel_bias_l[:, rel], NEG_INF)


def _ssm_kernel(u_ref, h0r_ref, h0i_ref, are_ref, aim_ref, ldt_ref, bre_ref, bim_ref, cre_ref, cim_ref,
                d_ref, wglu_ref, bglu_ref, y_ref, htr_ref, hti_ref,
                a_scr, bbar_scr, cmat_scr, h_scr, carry_scr, *, B, Tb, cw):
    step = pl.program_id(0)
    rows = Tb * B
    ts = SSM_TILE_STATES

    @pl.when(step == 0)
    def _():
        dt = jnp.exp(ldt_ref[...])
        ar, ai = are_ref[...], aim_ref[...]
        mag = jnp.exp(ar * dt)
        ang = ai * dt
        abr, abi = mag * jnp.cos(ang), mag * jnp.sin(ang)
        den = ar * ar + ai * ai
        nr, ni = abr - 1.0, abi
        cr = (nr * ar + ni * ai) / den
        ci = (ni * ar - nr * ai) / den
        for j in range(SSM_LANE_TILES):
            sl = slice(ts * j, ts * (j + 1))
            a_scr[j, :, 0:ts] = jnp.broadcast_to(abr[:, sl], (8, ts))
            a_scr[j, :, ts:2 * ts] = jnp.broadcast_to(abi[:, sl], (8, ts))
            br, bi = bre_ref[j], bim_ref[j]
            bbar_scr[j, :, 0:ts] = (cr[:, sl] * br - ci[:, sl] * bi).astype(BF16)
            bbar_scr[j, :, ts:2 * ts] = (cr[:, sl] * bi + ci[:, sl] * br).astype(BF16)
            cmat_scr[j, 0:ts, :] = cre_ref[j].astype(BF16)
            cmat_scr[j, ts:2 * ts, :] = (-cim_ref[j]).astype(BF16)
            carry_scr[j, :, 0:ts] = h0r_ref[:, sl]
            carry_scr[j, :, ts:2 * ts] = h0i_ref[:, sl]

    u = u_ref[...].reshape(rows, SSM_W)
    ub = u.astype(BF16)
    for j in range(SSM_LANE_TILES):
        h_scr[j] = jnp.dot(ub[:, LANE * j:LANE * (j + 1)], bbar_scr[j], preferred_element_type=F32)

    for j in range(SSM_LANE_TILES):
        for q in range(ts // cw):
            re_sl = slice(q * cw, (q + 1) * cw)
            im_sl = slice(ts + q * cw, ts + (q + 1) * cw)
            ar = jnp.broadcast_to(a_scr[j, 0:1, re_sl], (B, cw))
            ai = jnp.broadcast_to(a_scr[j, 0:1, im_sl], (B, cw))

            def body(t, c, j=j, re_sl=re_sl, im_sl=im_sl, ar=ar, ai=ai):
                hr, hi = c
                r0 = pl.multiple_of(t * B, B)
                nhr = ar * hr - ai * hi + h_scr[j, pl.ds(r0, B), re_sl]
                nhi = ar * hi + ai * hr + h_scr[j, pl.ds(r0, B), im_sl]
                h_scr[j, pl.ds(r0, B), re_sl] = nhr
                h_scr[j, pl.ds(r0, B), im_sl] = nhi
                return nhr, nhi

            hr, hi = lax.fori_loop(0, Tb, body, (carry_scr[j, :, re_sl], carry_scr[j, :, im_sl]))
            carry_scr[j, :, re_sl] = hr
            carry_scr[j, :, im_sl] = hi

    ys = [jnp.dot(h_scr[j].astype(BF16), cmat_scr[j], preferred_element_type=F32)
          for j in range(SSM_LANE_TILES)]
    y = jnp.concatenate(ys, axis=-1) + d_ref[...] * u
    g = jax.nn.gelu(y)
    gl = jnp.dot(g.astype(BF16), wglu_ref[...], preferred_element_type=F32) + bglu_ref[...]
    out = gl[:, :SSM_W] * jax.nn.sigmoid(gl[:, SSM_W:])
    y_ref[...] = out.reshape(Tb, B, SSM_W)

    @pl.when(step == pl.num_programs(0) - 1)
    def _():
        for j in range(SSM_LANE_TILES):
            sl = slice(ts * j, ts * (j + 1))
            htr_ref[:, sl] = carry_scr[j, :, 0:ts]
            hti_ref[:, sl] = carry_scr[j, :, ts:2 * ts]


def _ssm_call(u_tm, h0r, h0i, sp, wglu_bf16, bglu, *, Tb):
    L, B, _ = u_tm.shape
    rows = Tb * B
    cw = max(LANE, min(SSM_TILE_STATES, 4096 // B))
    kern = functools.partial(_ssm_kernel, B=B, Tb=Tb, cw=cw)
    full = lambda shape: pl.BlockSpec(shape, lambda i: (0,) * len(shape))
    nt, ts = SSM_LANE_TILES, SSM_TILE_STATES
    return pl.pallas_call(
        kern,
        grid=(L // Tb,),
        in_specs=[
            pl.BlockSpec((Tb, B, SSM_W), lambda i: (i, 0, 0)),
            full((B, N_STATES)), full((B, N_STATES)),
            full((1, N_STATES)), full((1, N_STATES)), full((1, N_STATES)),
            full((nt, LANE, ts)), full((nt, LANE, ts)),
            full((nt, ts, LANE)), full((nt, ts, LANE)),
            full((1, SSM_W)), full((SSM_W, 2 * SSM_W)), full((1, 2 * SSM_W)),
        ],
        out_specs=[
            pl.BlockSpec((Tb, B, SSM_W), lambda i: (i, 0, 0)),
            full((B, N_STATES)), full((B, N_STATES)),
        ],
        out_shape=[
            jax.ShapeDtypeStruct((L, B, SSM_W), F32),
            jax.ShapeDtypeStruct((B, N_STATES), F32),
            jax.ShapeDtypeStruct((B, N_STATES), F32),
        ],
        scratch_shapes=[
            pltpu.VMEM((nt, 8, 2 * ts), F32),
            pltpu.VMEM((nt, LANE, 2 * ts), BF16),
            pltpu.VMEM((nt, 2 * ts, LANE), BF16),
            pltpu.VMEM((nt, rows, 2 * ts), F32),
            pltpu.VMEM((nt, B, 2 * ts), F32),
        ],
        compiler_params=_params(("arbitrary",)),
        name="s5_scan",
    )(u_tm, h0r, h0i, sp["a_re"], sp["a_im"], sp["log_dt"], sp["b_re"], sp["b_im"],
      sp["c_re"], sp["c_im"], sp["d"], wglu_bf16, bglu)


def _ssm_layouts(a_re, a_im, log_dt, b_re, b_im, c_re, c_im, d_skip):
    gpt = SSM_GROUPS // SSM_LANE_TILES
    eye = jnp.eye(gpt, dtype=bool)[None, :, None, :, None]

    def blockdiag(m, rows, cols):
        return jnp.where(eye, m[:, :, :, None, :], 0.0).reshape(SSM_LANE_TILES, rows, cols)

    def b_tiles(b):
        bt = jnp.transpose(b, (0, 2, 1)).reshape(SSM_LANE_TILES, gpt, SSM_GC, SSM_STATE)
        return blockdiag(bt, LANE, SSM_TILE_STATES)

    def c_tiles(c):
        ct = jnp.transpose(c, (0, 2, 1)).reshape(SSM_LANE_TILES, gpt, SSM_STATE, SSM_GC)
        return blockdiag(ct, SSM_TILE_STATES, LANE)

    return {
        "a_re": a_re.reshape(1, N_STATES), "a_im": a_im.reshape(1, N_STATES),
        "log_dt": jnp.repeat(log_dt, SSM_STATE).reshape(1, N_STATES),
        "b_re": b_tiles(b_re), "b_im": b_tiles(b_im),
        "c_re": c_tiles(c_re), "c_im": c_tiles(c_im),
        "d": d_skip.reshape(1, SSM_W),
    }


def _pool_kernel(u_ref, halo_ref, hist_ref, w_ref, sc_ref, o_ref, *, tl, start_pos):
    i = pl.program_id(1)
    u = u_ref[0]
    halo = jnp.where(i == 0, hist_ref[0], halo_ref[0])
    ext = jnp.concatenate([halo, u], axis=0)
    sums = []
    s = ext
    for k in (1, 2, 4, 8):
        s = s + pltpu.roll(s, k, 0)
        sums.append(s)
    pos = start_pos + i * tl + lax.broadcasted_iota(jnp.int32, (tl, 1), 0)
    outs = []
    for g, w in enumerate(POOL_WINDOWS):
        ls = slice(g * POOL_GROUP_W, (g + 1) * POOL_GROUP_W)
        cnt = jnp.minimum(w, pos + 1).astype(F32)
        diff = sums[g][POOL_HALO:, ls] / cnt - u[:, ls]
        outs.append(jnp.dot(diff.astype(BF16), w_ref[g], preferred_element_type=F32))
    o_ref[0] = jnp.concatenate(outs, axis=-1) * sc_ref[...]


def _pool_call(proj, hist, w_bf16, scale, *, tl, start_pos):
    B, L, _ = proj.shape
    cb = COL_UPOOL // POOL_W
    hpb = tl // POOL_HALO
    kern = functools.partial(_pool_kernel, tl=tl, start_pos=start_pos)
    return pl.pallas_call(
        kern,
        grid=(B, L // tl),
        in_specs=[
            pl.BlockSpec((1, tl, POOL_W), lambda b, i: (b, i, cb)),
            pl.BlockSpec((1, POOL_HALO, POOL_W), lambda b, i: (b, jnp.maximum(i * hpb - 1, 0), cb)),
            pl.BlockSpec((1, POOL_HALO, POOL_W), lambda b, i: (b, 0, 0)),
            pl.BlockSpec((len(POOL_WINDOWS), POOL_GROUP_W, POOL_GROUP_W), lambda b, i: (0, 0, 0)),
            pl.BlockSpec((1, POOL_W), lambda b, i: (0, 0)),
        ],
        out_specs=pl.BlockSpec((1, tl, POOL_W), lambda b, i: (b, i, 0)),
        out_shape=jax.ShapeDtypeStruct((B, L, POOL_W), F32),
        compiler_params=_params(("arbitrary", "arbitrary")),
        name="pool_mix",
    )(proj, proj, hist, w_bf16, scale)


def _gated_norm(y, z, g):
    ms = jnp.mean(y * y, axis=-1, keepdims=True)
    return (y * lax.rsqrt(ms + EPS) * g * jax.nn.silu(z)).astype(BF16)


def _outproj_kernel(x_ref, ya_ref, ys_ref, yp_ref, za_ref, zs_ref, zp_ref, g_ref, gate_ref, w_ref, o_ref,
                    *, bb, tl):
    tm = bb * tl
    g = g_ref[...]
    ya = ya_ref[...].reshape(tm, ATT_W)
    ys_tm = ys_ref[...]
    ys = jnp.concatenate([ys_tm[:, b * SSM_W:(b + 1) * SSM_W] for b in range(bb)], axis=0)
    yp = yp_ref[...].reshape(tm, POOL_W)
    a_att = _gated_norm(ya, za_ref[...].reshape(tm, ATT_W), g[:, 0:ATT_W])
    a_ssm = _gated_norm(ys, zs_ref[...].reshape(tm, SSM_W), g[:, ATT_W:ATT_W + SSM_W])
    a_pool = _gated_norm(yp, zp_ref[...].reshape(tm, POOL_W), g[:, ATT_W + SSM_W:])
    acc = jnp.dot(a_att, w_ref[0:ATT_W, :], preferred_element_type=F32)
    acc = acc + jnp.dot(a_ssm, w_ref[ATT_W:ATT_W + SSM_W, :], preferred_element_type=F32)
    acc = acc + jnp.dot(a_pool, w_ref[ATT_W + SSM_W:, :], preferred_element_type=F32)
    o_ref[...] = x_ref[...] + gate_ref[...] * acc.reshape(bb, tl, D_MODEL)


def _outproj_call(x, y_att, y_ssm_tm, y_pool, proj, g, gate, w_bf16, *, bb, tl):
    B, L, _ = x.shape
    kern = functools.partial(_outproj_kernel, bb=bb, tl=tl)
    return pl.pallas_call(
        kern,
        grid=(B // bb, L // tl),
        in_specs=[
            pl.BlockSpec((bb, tl, D_MODEL), lambda b, i: (b, i, 0)),
            pl.BlockSpec((bb, tl, ATT_W), lambda b, i: (b, i, 0)),
            pl.BlockSpec((tl, bb * SSM_W), lambda b, i: (i, b)),
            pl.BlockSpec((bb, tl, POOL_W), lambda b, i: (b, i, 0)),
            pl.BlockSpec((bb, tl, ATT_W), lambda b, i: (b, i, COL_ZATT // ATT_W)),
            pl.BlockSpec((bb, tl, SSM_W), lambda b, i: (b, i, COL_ZSSM // SSM_W)),
            pl.BlockSpec((bb, tl, POOL_W), lambda b, i: (b, i, COL_ZPOOL // POOL_W)),
            pl.BlockSpec((1, D_MODEL), lambda b, i: (0, 0)),
            pl.BlockSpec((bb, 1, D_MODEL), lambda b, i: (b, 0, 0)),
            pl.BlockSpec((D_MODEL, D_MODEL), lambda b, i: (0, 0)),
        ],
        out_specs=pl.BlockSpec((bb, tl, D_MODEL), lambda b, i: (b, i, 0)),
        out_shape=jax.ShapeDtypeStruct((B, L, D_MODEL), F32),
        compiler_params=_params(("arbitrary", "arbitrary")),
        name="out_proj",
    )(x, y_att, y_ssm_tm, y_pool, proj, proj, proj, g, gate, w_bf16)


def _final_norm_kernel(x_ref, g_ref, o_ref):
    x = x_ref[...]
    ms = jnp.mean(x * x, axis=-1, keepdims=True)
    o_ref[...] = x * lax.rsqrt(ms + EPS) * g_ref[...]


def _final_norm_call(x, g):
    B, L, _ = x.shape
    rows = B * L
    tm = 512
    out = pl.pallas_call(
        _final_norm_kernel,
        grid=(rows // tm,),
        in_specs=[pl.BlockSpec((tm, D_MODEL), lambda i: (i, 0)),
                  pl.BlockSpec((1, D_MODEL), lambda i: (0, 0))],
        out_specs=pl.BlockSpec((tm, D_MODEL), lambda i: (i, 0)),
        out_shape=jax.ShapeDtypeStruct((rows, D_MODEL), F32),
        compiler_params=_params(("arbitrary",)),
        name="final_norm",
    )(x.reshape(rows, D_MODEL), g.reshape(1, D_MODEL))
    return out.reshape(B, L, D_MODEL)


def _layer(x, mod, k_prev, v_prev, h0r, h0i, pool_hist, lw, *, start_pos, hist, cfg):
    B, L, _ = x.shape
    shift, scale, gate = (mod[:, None, k * D_MODEL:(k + 1) * D_MODEL] for k in range(3))
    proj, u_tm = _inproj_call(x, shift, scale, lw["norm_g"], lw["w_in"], bb=cfg["in_bb"], tl=cfg["in_tl"])
    y_att = _attn_call(proj, k_prev, v_prev, lw["attn_bias"], tq=cfg["tq"], sq=cfg["sq"], hist=hist)
    y_ssm_tm, htr, hti = _ssm_call(u_tm.reshape(L, B, SSM_W), h0r, h0i, lw["ssm"], lw["w_glu"], lw["b_glu"],
                                   Tb=cfg["ssm_tb"])
    y_pool = _pool_call(proj, pool_hist, lw["w_pool"], lw["pool_scale"], tl=cfg["pool_tl"], start_pos=start_pos)
    x_new = _outproj_call(x, y_att, y_ssm_tm.reshape(L, B * SSM_W), y_pool, proj, lw["branch_g"], gate,
                          lw["w_out"], bb=cfg["out_bb"], tl=cfg["out_tl"])
    return x_new, proj, htr, hti


def kernel(x_prompt, x_sample, c_prompt, c_sample, cache_k, cache_v, state_ssm_re, state_ssm_im, state_pool, norm_g, w_ada, b_ada, w_in, rel_bias, ssm_a_re, ssm_a_im, ssm_log_dt, ssm_b_re, ssm_b_im, ssm_c_re, ssm_c_im, ssm_d, w_glu, b_glu, w_pool, pool_scale, branch_norm_g, w_out, final_norm_g):
    depth = w_in.shape[0]
    Bp, Lp, _ = x_prompt.shape
    Bs, Ls, _ = x_sample.shape
    past_len = PAST_LEN
    hist_s = cache_k.shape[2]
    assert Lp % PREV_ROWS == 0 and Ls % CHUNK == 0 and hist_s == PREV_ROWS and Ls >= POOL_HIST

    cfg_p = dict(in_bb=1, in_tl=512, tq=PREV_ROWS, sq=256, ssm_tb=512 // Bp, pool_tl=512, out_bb=1, out_tl=256)
    cfg_s = dict(in_bb=512 // Ls, in_tl=Ls, tq=Ls, sq=Ls, ssm_tb=512 // Bs, pool_tl=Ls, out_bb=256 // Ls, out_tl=Ls)

    mod = _ada_call(jnp.concatenate([c_prompt, c_sample], axis=0), w_ada, b_ada)

    xp, xs = x_prompt, x_sample
    zeros_state = jnp.zeros((Bp, N_STATES), F32)
    zeros_pool = jnp.zeros((Bp, POOL_HALO, POOL_W), F32)
    outs = [[] for _ in range(10)]
    for l in range(depth):
        lw = {
            "norm_g": norm_g[l].reshape(1, D_MODEL),
            "w_in": w_in[l].astype(BF16),
            "ssm": _ssm_layouts(ssm_a_re[l], ssm_a_im[l], ssm_log_dt[l], ssm_b_re[l], ssm_b_im[l],
                                ssm_c_re[l], ssm_c_im[l], ssm_d[l]),
            "w_glu": w_glu[l].astype(BF16),
            "b_glu": b_glu[l].reshape(1, 2 * SSM_W),
            "w_pool": w_pool[l].astype(BF16),
            "pool_scale": pool_scale[l].reshape(1, POOL_W),
            "branch_g": branch_norm_g[l].reshape(1, D_MODEL),
            "w_out": w_out[l].astype(BF16),
        }
        lw["attn_bias"] = _attn_bias(rel_bias[l], cfg_p["sq"])
        xp, proj_p, htr_p, hti_p = _layer(xp, mod[l, :Bp], None, None, zeros_state, zeros_state, zeros_pool, lw,
                                          start_pos=0, hist=0, cfg=cfg_p)
        lw["attn_bias"] = _attn_bias(rel_bias[l], cfg_s["sq"])
        pool_hist_s = jnp.pad(state_pool[l], ((0, 0), (POOL_HALO - POOL_HIST, 0), (0, 0)))
        xs, proj_s, htr_s, hti_s = _layer(
            xs, mod[l, Bp:], cache_k[l].reshape(Bs, hist_s, ATT_W), cache_v[l].reshape(Bs, hist_s, ATT_W),
            state_ssm_re[l].reshape(Bs, N_STATES), state_ssm_im[l].reshape(Bs, N_STATES), pool_hist_s, lw,
            start_pos=past_len, hist=hist_s, cfg=cfg_s)

        keep = min(PREV_ROWS, Lp)
        heads = lambda a: a.reshape(a.shape[0], a.shape[1], N_HEADS, HEAD_DIM)
        state = lambda a: a.reshape(a.shape[0], SSM_GROUPS, SSM_STATE)
        outs[0].append(heads(proj_p[:, Lp - keep:, COL_K:COL_K + ATT_W]))
        outs[1].append(heads(proj_p[:, Lp - keep:, COL_V:COL_V + ATT_W]))
        outs[2].append(state(htr_p))
        outs[3].append(state(hti_p))
        outs[4].append(proj_p[:, Lp - POOL_HIST:, COL_UPOOL:COL_UPOOL + POOL_W])
        outs[5].append(heads(proj_s[:, :, COL_K:COL_K + ATT_W]))
        outs[6].append(heads(proj_s[:, :, COL_V:COL_V + ATT_W]))
        outs[7].append(state(htr_s))
        outs[8].append(state(hti_s))
        outs[9].append(proj_s[:, Ls - POOL_HIST:, COL_UPOOL:COL_UPOOL + POOL_W])

    y_prompt = _final_norm_call(xp, final_norm_g)
    y_sample = _final_norm_call(xs, final_norm_g)
    return (y_prompt, y_sample) + tuple(jnp.stack(o) for o in outs)
```

```python
import functools

import jax
import jax.numpy as jnp
from jax import lax
from jax.experimental import pallas as pl
from jax.experimental.pallas import tpu as pltpu

F32 = jnp.float32
BF16 = jnp.bfloat16

D_MODEL = 2048
ATT_W = 1024
SSM_W = 512
POOL_W = 512
N_HEADS = 16
HEAD_DIM = 64
CHUNK = 64
PREV_ROWS = 512
BAND_CHUNKS = PREV_ROWS // CHUNK + 1
REL_CLIP = 256
PAST_LEN = 1024
SSM_GROUPS = 32
SSM_STATE = 64
SSM_GC = 16
N_STATES = SSM_GROUPS * SSM_STATE
POOL_WINDOWS = (2, 4, 8, 16)
POOL_GROUP_W = 128
POOL_HALO = 16
POOL_HIST = 15
IN_COLS = 6144
COL_Q, COL_K, COL_V, COL_ZATT = 0, 1024, 2048, 3072
COL_USSM, COL_ZSSM, COL_UPOOL, COL_ZPOOL = 4096, 4608, 5120, 5632
EPS = 1e-6
NEG_INF = -1e30

LANE = 128
SUBLANE = 8
SSM_LANE_TILES = SSM_W // LANE
SSM_TILE_STATES = N_STATES // SSM_LANE_TILES
VMEM_LIMIT = 56 * 1024 * 1024


def _params(sem):
    return pltpu.CompilerParams(dimension_semantics=sem, vmem_limit_bytes=VMEM_LIMIT)


def _resident(shape):
    return pl.BlockSpec(shape, lambda *_: (0,) * len(shape), pipeline_mode=pl.Buffered(1))


def _ada_kernel(c_ref, w_ref, b_ref, o_ref):
    a = jax.nn.silu(c_ref[...]).astype(BF16)
    o_ref[0] = jnp.dot(a, w_ref[0].astype(BF16), preferred_element_type=F32) + b_ref[0]


def _ada_call(c_all, w_ada, b_ada):
    depth = w_ada.shape[0]
    rows = c_all.shape[0]
    tn = 768
    return pl.pallas_call(
        _ada_kernel,
        grid=(depth, 3 * D_MODEL // tn),
        in_specs=[
            pl.BlockSpec((rows, D_MODEL), lambda l, j: (0, 0)),
            pl.BlockSpec((1, D_MODEL, tn), lambda l, j: (l, 0, j)),
            pl.BlockSpec((1, 1, tn), lambda l, j: (l, 0, j)),
        ],
        out_specs=pl.BlockSpec((1, rows, tn), lambda l, j: (l, 0, j)),
        out_shape=jax.ShapeDtypeStruct((depth, rows, 3 * D_MODEL), F32),
        compiler_params=_params(("arbitrary", "arbitrary")),
        name="ada_mod",
    )(c_all, w_ada, b_ada.reshape(depth, 1, 3 * D_MODEL))


def _inproj_kernel(x_ref, sh_ref, sc_ref, g_ref, w_ref, qkv_ref, z_ref, up_ref, utm_ref, kv_ref, h_scr,
                   *, bb, tl, kv_first):
    i = pl.program_id(1)
    tm = bb * tl
    g = g_ref[...]
    for b in range(bb):
        gs = jnp.broadcast_to(g * (1.0 + sc_ref[b]), (SUBLANE, D_MODEL))
        sh = jnp.broadcast_to(sh_ref[b], (SUBLANE, D_MODEL))

        def norm_rows(r, carry, b=b, gs=gs, sh=sh):
            r0 = pl.multiple_of(r * SUBLANE, SUBLANE)
            x = x_ref[b, pl.ds(r0, SUBLANE), :]
            ms = jnp.mean(x * x, axis=-1, keepdims=True)
            h_scr[pl.ds(b * tl + r0, SUBLANE), :] = x * lax.rsqrt(ms + EPS) * gs + sh
            return carry

        lax.fori_loop(0, tl // SUBLANE, norm_rows, 0)

    h = h_scr[...].astype(BF16)

    def proj(c0, width):
        return jnp.dot(h, w_ref[:, c0:c0 + width], preferred_element_type=F32)

    def rows3(a):
        return a.reshape(bb, tl, a.shape[-1])

    qkv_ref[:, :, 0:ATT_W] = rows3((proj(COL_Q, ATT_W) * (HEAD_DIM ** -0.5)).astype(BF16))
    k = proj(COL_K, ATT_W)
    qkv_ref[:, :, ATT_W:2 * ATT_W] = rows3(k.astype(BF16))
    v = proj(COL_V, ATT_W)
    qkv_ref[:, :, 2 * ATT_W:3 * ATT_W] = rows3(v.astype(BF16))

    @pl.when(i >= kv_first)
    def _():
        kv_ref[:, :, 0:ATT_W] = rows3(k)
        kv_ref[:, :, ATT_W:2 * ATT_W] = rows3(v)

    z_ref[:, :, 0:ATT_W] = rows3(proj(COL_ZATT, ATT_W).astype(BF16))
    z_ref[:, :, ATT_W:ATT_W + SSM_W] = rows3(proj(COL_ZSSM, SSM_W).astype(BF16))
    z_ref[:, :, ATT_W + SSM_W:] = rows3(proj(COL_ZPOOL, POOL_W).astype(BF16))
    up_ref[...] = rows3(proj(COL_UPOOL, POOL_W))
    us = proj(COL_USSM, SSM_W)
    for b in range(bb):
        utm_ref[:, b * SSM_W:(b + 1) * SSM_W] = us[b * tl:(b + 1) * tl, :]


def _inproj_call(x, shift, scale, g, w_bf16, *, bb, tl, keep):
    B, L, _ = x.shape
    kv_first = (L - keep) // tl
    kern = functools.partial(_inproj_kernel, bb=bb, tl=tl, kv_first=kv_first)
    rows = lambda w: pl.BlockSpec((bb, tl, w), lambda b, i: (b, i, 0))
    return pl.pallas_call(
        kern,
        grid=(B // bb, L // tl),
        in_specs=[
            rows(D_MODEL),
            pl.BlockSpec((bb, 1, D_MODEL), lambda b, i: (b, 0, 0)),
            pl.BlockSpec((bb, 1, D_MODEL), lambda b, i: (b, 0, 0)),
            _resident((1, D_MODEL)),
            _resident((D_MODEL, IN_COLS)),
        ],
        out_specs=[
            rows(3 * ATT_W), rows(D_MODEL), rows(POOL_W),
            pl.BlockSpec((tl, bb * SSM_W), lambda b, i: (i, b)),
            pl.BlockSpec((bb, tl, 2 * ATT_W), lambda b, i: (b, jnp.maximum(i - kv_first, 0), 0)),
        ],
        out_shape=[
            jax.ShapeDtypeStruct((B, L, 3 * ATT_W), BF16),
            jax.ShapeDtypeStruct((B, L, D_MODEL), BF16),
            jax.ShapeDtypeStruct((B, L, POOL_W), F32),
            jax.ShapeDtypeStruct((L, B * SSM_W), F32),
            jax.ShapeDtypeStruct((B, keep, 2 * ATT_W), F32),
        ],
        scratch_shapes=[pltpu.VMEM((bb * tl, D_MODEL), F32)],
        compiler_params=_params(("arbitrary", "arbitrary")),
        name="in_proj",
    )(x, shift, scale, g, w_bf16)


def _softmax_pv(scores, values):
    m = scores[0].max(axis=-1, keepdims=True)
    for sc in scores[1:]:
        m = jnp.maximum(m, sc.max(axis=-1, keepdims=True))
    den = None
    acc = None
    for sc, (val, transposed) in zip(scores, values):
        e = jnp.exp(sc - m)
        d = jnp.sum(e, axis=-1, keepdims=True)
        dims = (((1,), (1,)), ((), ())) if transposed else (((1,), (0,)), ((), ()))
        o = lax.dot_general(e.astype(BF16), val, dims, preferred_element_type=F32)
        den = d if den is None else den + d
        acc = o if acc is None else acc + o
    return acc / den


_NT = (((1,), (1,)), ((), ()))


def _attn_kernel(q_ref, kp_ref, kc_ref, vp_ref, vc_ref, bias_ref, o_ref, *, tq, sq):
    i = pl.program_id(2)

    def run(first_block):
        for s in range(tq // sq):
            q = q_ref[0, s * sq:(s + 1) * sq, :]
            pieces = [(kc_ref, vc_ref, 0, (s + 1) * sq, PREV_ROWS - s * sq)]
            if not first_block:
                pieces.insert(0, (kp_ref, vp_ref, s * sq, PREV_ROWS, 0))
            outs = []
            for hh in range(LANE // HEAD_DIM):
                hs = slice(hh * HEAD_DIM, (hh + 1) * HEAD_DIM)
                scores, values = [], []
                for kr, vr, a, b, c0 in pieces:
                    sc = lax.dot_general(q[:, hs], kr[0, a:b, hs], _NT, preferred_element_type=F32)
                    scores.append(sc + bias_ref[hh, :, c0:c0 + (b - a)])
                    values.append((vr[0, a:b, hs], False))
                outs.append(_softmax_pv(scores, values))
            o_ref[0, s * sq:(s + 1) * sq, :] = jnp.concatenate(outs, axis=-1).astype(BF16)

    @pl.when(i == 0)
    def _():
        run(True)

    @pl.when(i > 0)
    def _():
        run(False)


def _attn_call(qkv, bias, *, tq, sq):
    B, L, _ = qkv.shape
    assert tq == PREV_ROWS
    hp_blocks = ATT_W // LANE
    kb, vb = ATT_W // LANE, 2 * ATT_W // LANE
    kern = functools.partial(_attn_kernel, tq=tq, sq=sq)
    return pl.pallas_call(
        kern,
        grid=(B, hp_blocks, L // tq),
        in_specs=[
            pl.BlockSpec((1, tq, LANE), lambda b, h, i: (b, i, h)),
            pl.BlockSpec((1, tq, LANE), lambda b, h, i: (b, jnp.maximum(i - 1, 0), kb + h)),
            pl.BlockSpec((1, tq, LANE), lambda b, h, i: (b, i, kb + h)),
            pl.BlockSpec((1, tq, LANE), lambda b, h, i: (b, jnp.maximum(i - 1, 0), vb + h)),
            pl.BlockSpec((1, tq, LANE), lambda b, h, i: (b, i, vb + h)),
            pl.BlockSpec((LANE // HEAD_DIM, sq, PREV_ROWS + sq), lambda b, h, i: (h, 0, 0)),
        ],
        out_specs=pl.BlockSpec((1, tq, LANE), lambda b, h, i: (b, i, h)),
        out_shape=jax.ShapeDtypeStruct((B, L, ATT_W), BF16),
        compiler_params=_params(("arbitrary", "arbitrary", "arbitrary")),
        name="band_attn",
    )(qkv, qkv, qkv, qkv, qkv, bias)


def _attn_cached_kernel(qkv_ref, kt_ref, vt_ref, bias_ref, o_ref, *, tq):
    outs = []
    for h in range(N_HEADS):
        hs = slice(h * HEAD_DIM, (h + 1) * HEAD_DIM)
        q = qkv_ref[0, :, hs]
        k_new = qkv_ref[0, :, ATT_W + h * HEAD_DIM:ATT_W + (h + 1) * HEAD_DIM]
        v_new = qkv_ref[0, :, 2 * ATT_W + h * HEAD_DIM:2 * ATT_W + (h + 1) * HEAD_DIM]
        kt = kt_ref[0, 0, h].astype(BF16)
        vt = vt_ref[0, 0, h].astype(BF16)
        s_old = jnp.dot(q, kt, preferred_element_type=F32) + bias_ref[h, :, 0:PREV_ROWS]
        s_new = lax.dot_general(q, k_new, _NT, preferred_element_type=F32) + bias_ref[h, :, PREV_ROWS:]
        outs.append(_softmax_pv([s_old, s_new], [(vt, True), (v_new, False)]))
    o_ref[0] = jnp.concatenate(outs, axis=-1).astype(BF16)


def _attn_cached_call(qkv, kt_cache, vt_cache, layer, bias, *, tq):
    B, L, _ = qkv.shape
    assert L == tq == CHUNK
    kern = functools.partial(_attn_cached_kernel, tq=tq)
    cache_spec = pl.BlockSpec((1, 1, N_HEADS, HEAD_DIM, PREV_ROWS), lambda b: (layer, b, 0, 0, 0))
    return pl.pallas_call(
        kern,
        grid=(B,),
        in_specs=[
            pl.BlockSpec((1, tq, 3 * ATT_W), lambda b: (b, 0, 0)),
            cache_spec, cache_spec,
            _resident((N_HEADS, tq, PREV_ROWS + tq)),
        ],
        out_specs=pl.BlockSpec((1, tq, ATT_W), lambda b: (b, 0, 0)),
        out_shape=jax.ShapeDtypeStruct((B, L, ATT_W), BF16),
        compiler_params=_params(("arbitrary",)),
        name="cached_attn",
    )(qkv, kt_cache, vt_cache, bias)


def _attn_bias(rel_bias, sq):
    depth, H, _ = rel_bias.shape
    wl = PREV_ROWS + sq
    period = wl + sq + 1
    m = jnp.arange(period)
    m = jnp.where(m < wl, m, m - period)
    table = rel_bias[:, :, jnp.clip(PREV_ROWS - m, -REL_CLIP, REL_CLIP) + REL_CLIP]
    skew = jnp.tile(table, (1, 1, sq))[:, :, :sq * (period - 1)].reshape(depth, H, sq, period - 1)[..., :wl]
    i = jnp.arange(sq)[:, None]
    r = jnp.arange(wl)[None, :]
    lo = (i // CHUNK) * CHUNK
    band = (r >= lo) & (r < lo + BAND_CHUNKS * CHUNK)
    return jnp.where(band, skew, NEG_INF)


def _ssm_kernel(u_ref, h0r_ref, h0i_ref, are_ref, aim_ref, ldt_ref, bre_ref, bim_ref, cre_ref, cim_ref,
                d_ref, wglu_ref, bglu_ref, y_ref, htr_ref, hti_ref,
                a_scr, bbar_scr, cmat_scr, h_scr, carry_scr, *, B, Tb, cw):
    step = pl.program_id(0)
    rows = Tb * B
    ts = SSM_TILE_STATES

    @pl.when(step == 0)
    def _():
        dt = jnp.exp(ldt_ref[...])
        ar, ai = are_ref[...], aim_ref[...]
        mag = jnp.exp(ar * dt)
        ang = ai * dt
        abr, abi = mag * jnp.cos(ang), mag * jnp.sin(ang)
        den = ar * ar + ai * ai
        nr, ni = abr - 1.0, abi
        cr = (nr * ar + ni * ai) / den
        ci = (ni * ar - nr * ai) / den
        for j in range(SSM_LANE_TILES):
            sl = slice(ts * j, ts * (j + 1))
            a_scr[j, :, 0:ts] = jnp.broadcast_to(abr[:, sl], (SUBLANE, ts))
            a_scr[j, :, ts:2 * ts] = jnp.broadcast_to(abi[:, sl], (SUBLANE, ts))
            br, bi = bre_ref[j], bim_ref[j]
            bbar_scr[j, :, 0:ts] = (cr[:, sl] * br - ci[:, sl] * bi).astype(BF16)
            bbar_scr[j, :, ts:2 * ts] = (cr[:, sl] * bi + ci[:, sl] * br).astype(BF16)
            cmat_scr[j, 0:ts, :] = cre_ref[j].astype(BF16)
            cmat_scr[j, ts:2 * ts, :] = (-cim_ref[j]).astype(BF16)
            carry_scr[j, :, 0:ts] = h0r_ref[:, sl]
            carry_scr[j, :, ts:2 * ts] = h0i_ref[:, sl]

    u = u_ref[...].reshape(rows, SSM_W)
    ub = u.astype(BF16)
    for j in range(SSM_LANE_TILES):
        h_scr[j] = jnp.dot(ub[:, LANE * j:LANE * (j + 1)], bbar_scr[j], preferred_element_type=F32)

    for j in range(SSM_LANE_TILES):
        for q in range(ts // cw):
            re_sl = slice(q * cw, (q + 1) * cw)
            im_sl = slice(ts + q * cw, ts + (q + 1) * cw)
            ar = jnp.broadcast_to(a_scr[j, 0:1, re_sl], (B, cw))
            ai = jnp.broadcast_to(a_scr[j, 0:1, im_sl], (B, cw))

            def body(t, c, j=j, re_sl=re_sl, im_sl=im_sl, ar=ar, ai=ai):
                hr, hi = c
                r0 = pl.multiple_of(t * B, B)
                nhr = ar * hr - ai * hi + h_scr[j, pl.ds(r0, B), re_sl]
                nhi = ar * hi + ai * hr + h_scr[j, pl.ds(r0, B), im_sl]
                h_scr[j, pl.ds(r0, B), re_sl] = nhr
                h_scr[j, pl.ds(r0, B), im_sl] = nhi
                return nhr, nhi

            hr, hi = lax.fori_loop(0, Tb, body, (carry_scr[j, :, re_sl], carry_scr[j, :, im_sl]))
            carry_scr[j, :, re_sl] = hr
            carry_scr[j, :, im_sl] = hi

    ys = [jnp.dot(h_scr[j].astype(BF16), cmat_scr[j], preferred_element_type=F32)
          for j in range(SSM_LANE_TILES)]
    y = jnp.concatenate(ys, axis=-1) + d_ref[...] * u
    g = jax.nn.gelu(y)
    gl = jnp.dot(g.astype(BF16), wglu_ref[...], preferred_element_type=F32) + bglu_ref[...]
    out = gl[:, :SSM_W] * jax.nn.sigmoid(gl[:, SSM_W:])
    y_ref[...] = out.reshape(Tb, B, SSM_W)

    @pl.when(step == pl.num_programs(0) - 1)
    def _():
        for j in range(SSM_LANE_TILES):
            sl = slice(ts * j, ts * (j + 1))
            htr_ref[:, sl] = carry_scr[j, :, 0:ts]
            hti_ref[:, sl] = carry_scr[j, :, ts:2 * ts]


def _ssm_call(u_tm, h0r, h0i, sp, wglu_bf16, bglu, *, Tb):
    L, B, _ = u_tm.shape
    rows = Tb * B
    cw = max(LANE, min(SSM_TILE_STATES, 4096 // B))
    kern = functools.partial(_ssm_kernel, B=B, Tb=Tb, cw=cw)
    full = lambda shape: pl.BlockSpec(shape, lambda i: (0,) * len(shape))
    nt, ts = SSM_LANE_TILES, SSM_TILE_STATES
    return pl.pallas_call(
        kern,
        grid=(L // Tb,),
        in_specs=[
            pl.BlockSpec((Tb, B, SSM_W), lambda i: (i, 0, 0)),
            full((B, N_STATES)), full((B, N_STATES)),
            full((1, N_STATES)), full((1, N_STATES)), full((1, N_STATES)),
            full((nt, LANE, ts)), full((nt, LANE, ts)),
            full((nt, ts, LANE)), full((nt, ts, LANE)),
            full((1, SSM_W)), full((SSM_W, 2 * SSM_W)), full((1, 2 * SSM_W)),
        ],
        out_specs=[
            pl.BlockSpec((Tb, B, SSM_W), lambda i: (i, 0, 0)),
            full((B, N_STATES)), full((B, N_STATES)),
        ],
        out_shape=[
            jax.ShapeDtypeStruct((L, B, SSM_W), F32),
            jax.ShapeDtypeStruct((B, N_STATES), F32),
            jax.ShapeDtypeStruct((B, N_STATES), F32),
        ],
        scratch_shapes=[
            pltpu.VMEM((nt, SUBLANE, 2 * ts), F32),
            pltpu.VMEM((nt, LANE, 2 * ts), BF16),
            pltpu.VMEM((nt, 2 * ts, LANE), BF16),
            pltpu.VMEM((nt, rows, 2 * ts), F32),
            pltpu.VMEM((nt, B, 2 * ts), F32),
        ],
        compiler_params=_params(("arbitrary",)),
        name="s5_scan",
    )(u_tm, h0r, h0i, sp["a_re"], sp["a_im"], sp["log_dt"], sp["b_re"], sp["b_im"],
      sp["c_re"], sp["c_im"], sp["d"], wglu_bf16, bglu)


def _ssm_layouts(a_re, a_im, log_dt, b_re, b_im, c_re, c_im, d_skip):
    gpt = SSM_GROUPS // SSM_LANE_TILES
    eye = jnp.eye(gpt, dtype=bool)[None, :, None, :, None]

    def blockdiag(m, rows, cols):
        return jnp.where(eye, m[:, :, :, None, :], 0.0).reshape(SSM_LANE_TILES, rows, cols)

    def b_tiles(b):
        bt = jnp.transpose(b, (0, 2, 1)).reshape(SSM_LANE_TILES, gpt, SSM_GC, SSM_STATE)
        return blockdiag(bt, LANE, SSM_TILE_STATES)

    def c_tiles(c):
        ct = jnp.transpose(c, (0, 2, 1)).reshape(SSM_LANE_TILES, gpt, SSM_STATE, SSM_GC)
        return blockdiag(ct, SSM_TILE_STATES, LANE)

    return {
        "a_re": a_re.reshape(1, N_STATES), "a_im": a_im.reshape(1, N_STATES),
        "log_dt": jnp.repeat(log_dt, SSM_STATE).reshape(1, N_STATES),
        "b_re": b_tiles(b_re), "b_im": b_tiles(b_im),
        "c_re": c_tiles(c_re), "c_im": c_tiles(c_im),
        "d": d_skip.reshape(1, SSM_W),
    }


def _pool_kernel(u_ref, halo_ref, hist_ref, w_ref, sc_ref, o_ref, *, tl, start_pos):
    i = pl.program_id(1)
    u = u_ref[0]
    halo = jnp.where(i == 0, hist_ref[0], halo_ref[0])
    ext = jnp.concatenate([halo, u], axis=0)
    sums = []
    s = ext
    for k in (1, 2, 4, 8):
        s = s + pltpu.roll(s, k, 0)
        sums.append(s)
    pos = start_pos + i * tl + lax.broadcasted_iota(jnp.int32, (tl, 1), 0)
    outs = []
    for g, w in enumerate(POOL_WINDOWS):
        ls = slice(g * POOL_GROUP_W, (g + 1) * POOL_GROUP_W)
        cnt = jnp.minimum(w, pos + 1).astype(F32)
        diff = sums[g][POOL_HALO:, ls] / cnt - u[:, ls]
        outs.append(jnp.dot(diff.astype(BF16), w_ref[g], preferred_element_type=F32))
    o_ref[0] = (jnp.concatenate(outs, axis=-1) * sc_ref[...]).astype(BF16)


def _pool_call(u_pool, hist, w_bf16, scale, *, tl, start_pos):
    B, L, _ = u_pool.shape
    hpb = tl // POOL_HALO
    kern = functools.partial(_pool_kernel, tl=tl, start_pos=start_pos)
    return pl.pallas_call(
        kern,
        grid=(B, L // tl),
        in_specs=[
            pl.BlockSpec((1, tl, POOL_W), lambda b, i: (b, i, 0)),
            pl.BlockSpec((1, POOL_HALO, POOL_W), lambda b, i: (b, jnp.maximum(i * hpb - 1, 0), 0)),
            pl.BlockSpec((1, POOL_HALO, POOL_W), lambda b, i: (b, 0, 0)),
            pl.BlockSpec((len(POOL_WINDOWS), POOL_GROUP_W, POOL_GROUP_W), lambda b, i: (0, 0, 0)),
            pl.BlockSpec((1, POOL_W), lambda b, i: (0, 0)),
        ],
        out_specs=pl.BlockSpec((1, tl, POOL_W), lambda b, i: (b, i, 0)),
        out_shape=jax.ShapeDtypeStruct((B, L, POOL_W), BF16),
        compiler_params=_params(("arbitrary", "arbitrary")),
        name="pool_mix",
    )(u_pool, u_pool, hist, w_bf16, scale)


def _gated_norm(y, z, g):
    y = y.astype(F32)
    ms = jnp.mean(y * y, axis=-1, keepdims=True)
    return (y * lax.rsqrt(ms + EPS) * g * jax.nn.silu(z.astype(F32))).astype(BF16)


def _outproj_kernel(x_ref, ya_ref, ys_ref, yp_ref, z_ref, g_ref, gate_ref, w_ref, o_ref, *, bb, tl):
    tm = bb * tl
    g = g_ref[...]
    z = z_ref[...].reshape(tm, D_MODEL)
    ya = ya_ref[...].reshape(tm, ATT_W)
    ys_tm = ys_ref[...]
    ys = jnp.concatenate([ys_tm[:, b * SSM_W:(b + 1) * SSM_W] for b in range(bb)], axis=0)
    yp = yp_ref[...].reshape(tm, POOL_W)
    a_att = _gated_norm(ya, z[:, 0:ATT_W], g[:, 0:ATT_W])
    a_ssm = _gated_norm(ys, z[:, ATT_W:ATT_W + SSM_W], g[:, ATT_W:ATT_W + SSM_W])
    a_pool = _gated_norm(yp, z[:, ATT_W + SSM_W:], g[:, ATT_W + SSM_W:])
    acc = jnp.dot(a_att, w_ref[0:ATT_W, :], preferred_element_type=F32)
    acc = acc + jnp.dot(a_ssm, w_ref[ATT_W:ATT_W + SSM_W, :], preferred_element_type=F32)
    acc = acc + jnp.dot(a_pool, w_ref[ATT_W + SSM_W:, :], preferred_element_type=F32)
    o_ref[...] = x_ref[...] + gate_ref[...] * acc.reshape(bb, tl, D_MODEL)


def _outproj_call(x, y_att, y_ssm_tm, y_pool, z, g, gate, w_bf16, *, bb, tl):
    B, L, _ = x.shape
    kern = functools.partial(_outproj_kernel, bb=bb, tl=tl)
    rows = lambda w: pl.BlockSpec((bb, tl, w), lambda b, i: (b, i, 0))
    return pl.pallas_call(
        kern,
        grid=(B // bb, L // tl),
        in_specs=[
            rows(D_MODEL), rows(ATT_W),
            pl.BlockSpec((tl, bb * SSM_W), lambda b, i: (i, b)),
            rows(POOL_W), rows(D_MODEL),
            _resident((1, D_MODEL)),
            pl.BlockSpec((bb, 1, D_MODEL), lambda b, i: (b, 0, 0)),
            _resident((D_MODEL, D_MODEL)),
        ],
        out_specs=rows(D_MODEL),
        out_shape=jax.ShapeDtypeStruct((B, L, D_MODEL), F32),
        compiler_params=_params(("arbitrary", "arbitrary")),
        name="out_proj",
    )(x, y_att, y_ssm_tm, y_pool, z, g, gate, w_bf16)


def _final_norm_kernel(x_ref, g_ref, o_ref):
    x = x_ref[...]
    ms = jnp.mean(x * x, axis=-1, keepdims=True)
    o_ref[...] = x * lax.rsqrt(ms + EPS) * g_ref[...]


def _final_norm_call(x, g):
    B, L, _ = x.shape
    rows = B * L
    tm = 512
    out = pl.pallas_call(
        _final_norm_kernel,
        grid=(rows // tm,),
        in_specs=[pl.BlockSpec((tm, D_MODEL), lambda i: (i, 0)),
                  pl.BlockSpec((1, D_MODEL), lambda i: (0, 0))],
        out_specs=pl.BlockSpec((tm, D_MODEL), lambda i: (i, 0)),
        out_shape=jax.ShapeDtypeStruct((rows, D_MODEL), F32),
        compiler_params=_params(("arbitrary",)),
        name="final_norm",
    )(x.reshape(rows, D_MODEL), g.reshape(1, D_MODEL))
    return out.reshape(B, L, D_MODEL)


def _layer(x, mod, attend, h0r, h0i, pool_hist, lw, *, start_pos, keep, cfg):
    B, L, _ = x.shape
    shift, scale, gate = (mod[:, None, k * D_MODEL:(k + 1) * D_MODEL] for k in range(3))
    qkv, z, u_pool, u_tm, kv = _inproj_call(x, shift, scale, lw["norm_g"], lw["w_in"],
                                            bb=cfg["in_bb"], tl=cfg["in_tl"], keep=keep)
    y_att = attend(qkv)
    y_ssm_tm, htr, hti = _ssm_call(u_tm.reshape(L, B, SSM_W), h0r, h0i, lw["ssm"], lw["w_glu"], lw["b_glu"],
                                   Tb=cfg["ssm_tb"])
    y_pool = _pool_call(u_pool, pool_hist, lw["w_pool"], lw["pool_scale"], tl=cfg["pool_tl"], start_pos=start_pos)
    x_new = _outproj_call(x, y_att, y_ssm_tm.reshape(L, B * SSM_W), y_pool, z, lw["branch_g"], gate,
                          lw["w_out"], bb=cfg["out_bb"], tl=cfg["out_tl"])
    return x_new, kv, u_pool, htr, hti


def kernel(x_prompt, x_sample, c_prompt, c_sample, cache_k, cache_v, state_ssm_re, state_ssm_im, state_pool, norm_g, w_ada, b_ada, w_in, rel_bias, ssm_a_re, ssm_a_im, ssm_log_dt, ssm_b_re, ssm_b_im, ssm_c_re, ssm_c_im, ssm_d, w_glu, b_glu, w_pool, pool_scale, branch_norm_g, w_out, final_norm_g):
    depth = w_in.shape[0]
    Bp, Lp, _ = x_prompt.shape
    Bs, Ls, _ = x_sample.shape
    hist_s = cache_k.shape[2]
    assert Lp % PREV_ROWS == 0 and Ls == CHUNK and hist_s == PREV_ROWS and Ls >= POOL_HIST
    keep = min(PREV_ROWS, Lp)

    cfg_p = dict(in_bb=1, in_tl=256, tq=PREV_ROWS, sq=256, ssm_tb=512 // Bp, pool_tl=512, out_bb=1, out_tl=256)
    cfg_s = dict(in_bb=256 // Ls, in_tl=Ls, tq=Ls, ssm_tb=512 // Bs, pool_tl=Ls, out_bb=256 // Ls, out_tl=Ls)

    mod = _ada_call(jnp.concatenate([c_prompt, c_sample], axis=0), w_ada, b_ada)
    bias_p = _attn_bias(rel_bias, cfg_p["sq"])
    bias_s = _attn_bias(rel_bias, Ls)
    kt_cache = jnp.transpose(cache_k, (0, 1, 3, 4, 2))
    vt_cache = jnp.transpose(cache_v, (0, 1, 3, 4, 2))
    w_in_b, w_out_b, w_glu_b, w_pool_b = (w.astype(BF16) for w in (w_in, w_out, w_glu, w_pool))

    xp, xs = x_prompt, x_sample
    zeros_state = jnp.zeros((Bp, N_STATES), F32)
    zeros_pool = jnp.zeros((Bp, POOL_HALO, POOL_W), F32)
    outs = [[] for _ in range(10)]
    for l in range(depth):
        lw = {
            "norm_g": norm_g[l].reshape(1, D_MODEL),
            "w_in": w_in_b[l],
            "ssm": _ssm_layouts(ssm_a_re[l], ssm_a_im[l], ssm_log_dt[l], ssm_b_re[l], ssm_b_im[l],
                                ssm_c_re[l], ssm_c_im[l], ssm_d[l]),
            "w_glu": w_glu_b[l],
            "b_glu": b_glu[l].reshape(1, 2 * SSM_W),
            "w_pool": w_pool_b[l],
            "pool_scale": pool_scale[l].reshape(1, POOL_W),
            "branch_g": branch_norm_g[l].reshape(1, D_MODEL),
            "w_out": w_out_b[l],
        }
        attend_p = functools.partial(_attn_call, bias=bias_p[l], tq=cfg_p["tq"], sq=cfg_p["sq"])
        xp, kv_p, up_p, htr_p, hti_p = _layer(xp, mod[l, :Bp], attend_p, zeros_state, zeros_state, zeros_pool, lw,
                                              start_pos=0, keep=keep, cfg=cfg_p)
        attend_s = functools.partial(_attn_cached_call, kt_cache=kt_cache, vt_cache=vt_cache, layer=l,
                                     bias=bias_s[l], tq=cfg_s["tq"])
        pool_hist_s = jnp.pad(state_pool[l], ((0, 0), (POOL_HALO - POOL_HIST, 0), (0, 0)))
        xs, kv_s, up_s, htr_s, hti_s = _layer(
            xs, mod[l, Bp:], attend_s, state_ssm_re[l].reshape(Bs, N_STATES), state_ssm_im[l].reshape(Bs, N_STATES),
            pool_hist_s, lw, start_pos=PAST_LEN, keep=Ls, cfg=cfg_s)

        heads = lambda a: a.reshape(a.shape[0], a.shape[1], N_HEADS, HEAD_DIM)
        state = lambda a: a.reshape(a.shape[0], SSM_GROUPS, SSM_STATE)
        outs[0].append(heads(kv_p[:, :, :ATT_W]))
        outs[1].append(heads(kv_p[:, :, ATT_W:]))
        outs[2].append(state(htr_p))
        outs[3].append(state(hti_p))
        outs[4].append(up_p[:, Lp - POOL_HIST:])
        outs[5].append(heads(kv_s[:, :, :ATT_W]))
        outs[6].append(heads(kv_s[:, :, ATT_W:]))
        outs[7].append(state(htr_s))
        outs[8].append(state(hti_s))
        outs[9].append(up_s[:, Ls - POOL_HIST:])

    y_prompt = _final_norm_call(xp, final_norm_g)
    y_sample = _final_norm_call(xs, final_norm_g)
    return (y_prompt, y_sample) + tuple(jnp.stack(o) for o in outs)
```

```python
import functools

import jax
import jax.numpy as jnp
from jax import lax
from jax.experimental import pallas as pl
from jax.experimental.pallas import tpu as pltpu

F32 = jnp.float32
BF16 = jnp.bfloat16

D_MODEL = 2048
ATT_W = 1024
SSM_W = 512
POOL_W = 512
N_HEADS = 16
HEAD_DIM = 64
CHUNK = 64
PREV_ROWS = 512
BAND_CHUNKS = PREV_ROWS // CHUNK + 1
REL_CLIP = 256
PAST_LEN = 1024
SSM_GROUPS = 32
SSM_STATE = 64
SSM_GC = 16
N_STATES = SSM_GROUPS * SSM_STATE
POOL_WINDOWS = (2, 4, 8, 16)
POOL_GROUP_W = 128
POOL_HALO = 16
POOL_HIST = 15
IN_COLS = 6144
COL_Q, COL_K, COL_V, COL_ZATT = 0, 1024, 2048, 3072
COL_USSM, COL_ZSSM, COL_UPOOL, COL_ZPOOL = 4096, 4608, 5120, 5632
EPS = 1e-6
NEG_INF = -1e30

LANE = 128
SUBLANE = 8
SSM_LANE_TILES = SSM_W // LANE
SSM_TILE_STATES = N_STATES // SSM_LANE_TILES
VMEM_LIMIT = 56 * 1024 * 1024


def _params(sem):
    return pltpu.CompilerParams(dimension_semantics=sem, vmem_limit_bytes=VMEM_LIMIT)


def _resident(shape):
    return pl.BlockSpec(shape, lambda *_: (0,) * len(shape), pipeline_mode=pl.Buffered(1))


def _ada_kernel(c_ref, w_ref, b_ref, o_ref):
    a = jax.nn.silu(c_ref[...]).astype(BF16)
    o_ref[0] = jnp.dot(a, w_ref[0].astype(BF16), preferred_element_type=F32) + b_ref[0]


def _ada_call(c_all, w_ada, b_ada):
    depth = w_ada.shape[0]
    rows = c_all.shape[0]
    tn = 768
    return pl.pallas_call(
        _ada_kernel,
        grid=(depth, 3 * D_MODEL // tn),
        in_specs=[
            pl.BlockSpec((rows, D_MODEL), lambda l, j: (0, 0)),
            pl.BlockSpec((1, D_MODEL, tn), lambda l, j: (l, 0, j)),
            pl.BlockSpec((1, 1, tn), lambda l, j: (l, 0, j)),
        ],
        out_specs=pl.BlockSpec((1, rows, tn), lambda l, j: (l, 0, j)),
        out_shape=jax.ShapeDtypeStruct((depth, rows, 3 * D_MODEL), F32),
        compiler_params=_params(("arbitrary", "arbitrary")),
        name="ada_mod",
    )(c_all, w_ada, b_ada.reshape(depth, 1, 3 * D_MODEL))


def _inproj_kernel(x_ref, sh_ref, sc_ref, g_ref, w_ref, qkv_ref, z_ref, up_ref, utm_ref, kv_ref, h_scr,
                   *, bb, tl, kv_first):
    i = pl.program_id(1)
    g = g_ref[...]
    lane_tiles = [slice(k * LANE, (k + 1) * LANE) for k in range(D_MODEL // LANE)]
    for b in range(bb):
        acc = None
        for sl in lane_tiles:
            xk = x_ref[b, :, sl]
            acc = xk * xk if acc is None else acc + xk * xk
        ms = jnp.sum(acc, axis=-1, keepdims=True) * (1.0 / D_MODEL)
        rs = lax.rsqrt(ms + EPS)
        gs = g * (1.0 + sc_ref[b])
        sh = sh_ref[b]
        for sl in lane_tiles:
            h_scr[b * tl:(b + 1) * tl, sl] = (x_ref[b, :, sl] * rs * gs[:, sl] + sh[:, sl]).astype(BF16)

    h = h_scr[...]

    def proj(c0, width):
        return jnp.dot(h, w_ref[:, c0:c0 + width], preferred_element_type=F32)

    def rows3(a):
        return a.reshape(bb, tl, a.shape[-1])

    qkv_ref[:, :, 0:ATT_W] = rows3((proj(COL_Q, ATT_W) * (HEAD_DIM ** -0.5)).astype(BF16))
    k = proj(COL_K, ATT_W)
    qkv_ref[:, :, ATT_W:2 * ATT_W] = rows3(k.astype(BF16))
    v = proj(COL_V, ATT_W)
    qkv_ref[:, :, 2 * ATT_W:3 * ATT_W] = rows3(v.astype(BF16))

    @pl.when(i >= kv_first)
    def _():
        kv_ref[:, :, 0:ATT_W] = rows3(k)
        kv_ref[:, :, ATT_W:2 * ATT_W] = rows3(v)

    z_ref[:, :, 0:ATT_W] = rows3(proj(COL_ZATT, ATT_W).astype(BF16))
    z_ref[:, :, ATT_W:ATT_W + SSM_W] = rows3(proj(COL_ZSSM, SSM_W).astype(BF16))
    z_ref[:, :, ATT_W + SSM_W:] = rows3(proj(COL_ZPOOL, POOL_W).astype(BF16))
    up_ref[...] = rows3(proj(COL_UPOOL, POOL_W))
    us = proj(COL_USSM, SSM_W)
    for b in range(bb):
        utm_ref[:, b * SSM_W:(b + 1) * SSM_W] = us[b * tl:(b + 1) * tl, :]


def _inproj_call(x, shift, scale, g, w_bf16, *, bb, tl, keep):
    B, L, _ = x.shape
    kv_first = (L - keep) // tl
    kern = functools.partial(_inproj_kernel, bb=bb, tl=tl, kv_first=kv_first)
    rows = lambda w: pl.BlockSpec((bb, tl, w), lambda b, i: (b, i, 0))
    return pl.pallas_call(
        kern,
        grid=(B // bb, L // tl),
        in_specs=[
            rows(D_MODEL),
            pl.BlockSpec((bb, 1, D_MODEL), lambda b, i: (b, 0, 0)),
            pl.BlockSpec((bb, 1, D_MODEL), lambda b, i: (b, 0, 0)),
            _resident((1, D_MODEL)),
            _resident((D_MODEL, IN_COLS)),
        ],
        out_specs=[
            rows(3 * ATT_W), rows(D_MODEL), rows(POOL_W),
            pl.BlockSpec((tl, bb * SSM_W), lambda b, i: (i, b)),
            pl.BlockSpec((bb, tl, 2 * ATT_W), lambda b, i: (b, jnp.maximum(i - kv_first, 0), 0)),
        ],
        out_shape=[
            jax.ShapeDtypeStruct((B, L, 3 * ATT_W), BF16),
            jax.ShapeDtypeStruct((B, L, D_MODEL), BF16),
            jax.ShapeDtypeStruct((B, L, POOL_W), F32),
            jax.ShapeDtypeStruct((L, B * SSM_W), F32),
            jax.ShapeDtypeStruct((B, keep, 2 * ATT_W), F32),
        ],
        scratch_shapes=[pltpu.VMEM((bb * tl, D_MODEL), BF16)],
        compiler_params=_params(("arbitrary", "arbitrary")),
        name="in_proj",
    )(x, shift, scale, g, w_bf16)


def _softmax_pv(scores, values):
    m = scores[0].max(axis=-1, keepdims=True)
    for sc in scores[1:]:
        m = jnp.maximum(m, sc.max(axis=-1, keepdims=True))
    den = None
    acc = None
    for sc, (val, transposed) in zip(scores, values):
        e = jnp.exp(sc - m)
        d = jnp.sum(e, axis=-1, keepdims=True)
        dims = (((1,), (1,)), ((), ())) if transposed else (((1,), (0,)), ((), ()))
        o = lax.dot_general(e.astype(BF16), val, dims, preferred_element_type=F32)
        den = d if den is None else den + d
        acc = o if acc is None else acc + o
    return acc / den


_NT = (((1,), (1,)), ((), ()))


def _attn_kernel(q_ref, kp_ref, kc_ref, vp_ref, vc_ref, bias_ref, o_ref, *, tq, sq):
    i = pl.program_id(2)
    lane = lax.broadcasted_iota(jnp.int32, (1, LANE), 1)
    own = [lane < HEAD_DIM, lane >= HEAD_DIM]

    def run(first_block):
        def pieces_of(s):
            pieces = [(kc_ref, vc_ref, 0, (s + 1) * sq, PREV_ROWS - s * sq)]
            if not first_block:
                pieces.insert(0, (kp_ref, vp_ref, s * sq, PREV_ROWS, 0))
            return pieces

        def scores_of(s, hh):
            q = q_ref[0, s * sq:(s + 1) * sq, :]
            qh = jnp.where(own[hh], q, jnp.zeros_like(q))
            return [lax.dot_general(qh, kr[0, a:b, :], _NT, preferred_element_type=F32)
                    + bias_ref[hh, :, c0:c0 + (b - a)] for kr, _, a, b, c0 in pieces_of(s)]

        def weighted_values(s, hh, scores):
            m = scores[0].max(axis=-1, keepdims=True)
            for sc in scores[1:]:
                m = jnp.maximum(m, sc.max(axis=-1, keepdims=True))
            o = None
            for sc, (_, vr, a, b, _) in zip(scores, pieces_of(s)):
                v = vr[0, a:b, :]
                vh = jnp.where(own[hh], v, jnp.ones_like(v))
                pv = jnp.dot(jnp.exp(sc - m).astype(BF16), vh, preferred_element_type=F32)
                o = pv if o is None else o + pv
            return o

        chains = [(s, hh) for s in range(tq // sq) for hh in range(LANE // HEAD_DIM)]
        scores, acc = {chains[0]: scores_of(*chains[0])}, {}
        for n, c in enumerate(chains):
            if n + 1 < len(chains):
                scores[chains[n + 1]] = scores_of(*chains[n + 1])
            acc[c] = weighted_values(*c, scores.pop(c))
        for s in range(tq // sq):
            a0, a1 = acc[(s, 0)], acc[(s, 1)]
            num = jnp.where(own[0], a0, a1)
            den = pltpu.roll(jnp.where(own[0], a1, a0), HEAD_DIM, 1)
            o_ref[0, s * sq:(s + 1) * sq, :] = (num / den).astype(BF16)

    @pl.when(i == 0)
    def _():
        run(True)

    @pl.when(i > 0)
    def _():
        run(False)


def _attn_call(qkv, bias, *, tq, sq):
    B, L, _ = qkv.shape
    assert tq == PREV_ROWS
    hp_blocks = ATT_W // LANE
    kb, vb = ATT_W // LANE, 2 * ATT_W // LANE
    kern = functools.partial(_attn_kernel, tq=tq, sq=sq)
    return pl.pallas_call(
        kern,
        grid=(B, hp_blocks, L // tq),
        in_specs=[
            pl.BlockSpec((1, tq, LANE), lambda b, h, i: (b, i, h)),
            pl.BlockSpec((1, tq, LANE), lambda b, h, i: (b, jnp.maximum(i - 1, 0), kb + h)),
            pl.BlockSpec((1, tq, LANE), lambda b, h, i: (b, i, kb + h)),
            pl.BlockSpec((1, tq, LANE), lambda b, h, i: (b, jnp.maximum(i - 1, 0), vb + h)),
            pl.BlockSpec((1, tq, LANE), lambda b, h, i: (b, i, vb + h)),
            pl.BlockSpec((LANE // HEAD_DIM, sq, PREV_ROWS + sq), lambda b, h, i: (h, 0, 0)),
        ],
        out_specs=pl.BlockSpec((1, tq, LANE), lambda b, h, i: (b, i, h)),
        out_shape=jax.ShapeDtypeStruct((B, L, ATT_W), BF16),
        compiler_params=_params(("arbitrary", "arbitrary", "arbitrary")),
        name="band_attn",
    )(qkv, qkv, qkv, qkv, qkv, bias)


def _attn_cached_kernel(qkv_ref, kt_ref, vt_ref, bias_ref, o_ref, *, tq):
    def scores_of(h):
        hs = slice(h * HEAD_DIM, (h + 1) * HEAD_DIM)
        q = qkv_ref[0, :, hs]
        k_new = qkv_ref[0, :, ATT_W + h * HEAD_DIM:ATT_W + (h + 1) * HEAD_DIM]
        kt = kt_ref[0, 0, h].astype(BF16)
        s_old = jnp.dot(q, kt, preferred_element_type=F32) + bias_ref[h, :, 0:PREV_ROWS]
        s_new = lax.dot_general(q, k_new, _NT, preferred_element_type=F32) + bias_ref[h, :, PREV_ROWS:]
        return [s_old, s_new]

    def weighted_values(h, scores):
        v_new = qkv_ref[0, :, 2 * ATT_W + h * HEAD_DIM:2 * ATT_W + (h + 1) * HEAD_DIM]
        vt = vt_ref[0, 0, h].astype(BF16)
        return _softmax_pv(scores, [(vt, True), (v_new, False)])

    ahead = 4
    scores = {h: scores_of(h) for h in range(ahead)}
    outs = []
    for h in range(N_HEADS):
        if h + ahead < N_HEADS:
            scores[h + ahead] = scores_of(h + ahead)
        outs.append(weighted_values(h, scores.pop(h)))
    o_ref[0] = jnp.concatenate(outs, axis=-1).astype(BF16)


def _attn_cached_call(qkv, kt_cache, vt_cache, layer, bias, *, tq):
    B, L, _ = qkv.shape
    assert L == tq == CHUNK
    kern = functools.partial(_attn_cached_kernel, tq=tq)
    cache_spec = pl.BlockSpec((1, 1, N_HEADS, HEAD_DIM, PREV_ROWS), lambda b: (layer, b, 0, 0, 0))
    return pl.pallas_call(
        kern,
        grid=(B,),
        in_specs=[
            pl.BlockSpec((1, tq, 3 * ATT_W), lambda b: (b, 0, 0)),
            cache_spec, cache_spec,
            _resident((N_HEADS, tq, PREV_ROWS + tq)),
        ],
        out_specs=pl.BlockSpec((1, tq, ATT_W), lambda b: (b, 0, 0)),
        out_shape=jax.ShapeDtypeStruct((B, L, ATT_W), BF16),
        compiler_params=_params(("arbitrary",)),
        name="cached_attn",
    )(qkv, kt_cache, vt_cache, bias)


def _attn_bias(rel_bias, sq):
    depth, H, _ = rel_bias.shape
    wl = PREV_ROWS + sq
    period = wl + sq + 1
    m = jnp.arange(period)
    m = jnp.where(m < wl, m, m - period)
    table = rel_bias[:, :, jnp.clip(PREV_ROWS - m, -REL_CLIP, REL_CLIP) + REL_CLIP]
    skew = jnp.tile(table, (1, 1, sq))[:, :, :sq * (period - 1)].reshape(depth, H, sq, period - 1)[..., :wl]
    i = jnp.arange(sq)[:, None]
    r = jnp.arange(wl)[None, :]
    lo = (i // CHUNK) * CHUNK
    band = (r >= lo) & (r < lo + BAND_CHUNKS * CHUNK)
    return jnp.where(band, skew, NEG_INF)


def _ssm_kernel(u_ref, h0r_ref, h0i_ref, are_ref, aim_ref, ldt_ref, bre_ref, bim_ref, cre_ref, cim_ref,
                d_ref, wglu_ref, bglu_ref, y_ref, htr_ref, hti_ref,
                a_scr, bbar_scr, cmat_scr, h_scr, carry_scr, *, B, Tb, cw):
    step = pl.program_id(0)
    rows = Tb * B
    ts = SSM_TILE_STATES

    @pl.when(step == 0)
    def _():
        dt = jnp.exp(ldt_ref[...])
        ar, ai = are_ref[...], aim_ref[...]
        mag = jnp.exp(ar * dt)
        ang = ai * dt
        abr, abi = mag * jnp.cos(ang), mag * jnp.sin(ang)
        den = ar * ar + ai * ai
        nr, ni = abr - 1.0, abi
        cr = (nr * ar + ni * ai) / den
        ci = (ni * ar - nr * ai) / den
        for j in range(SSM_LANE_TILES):
            sl = slice(ts * j, ts * (j + 1))
            a_scr[j, :, 0:ts] = jnp.broadcast_to(abr[:, sl], (SUBLANE, ts))
            a_scr[j, :, ts:2 * ts] = jnp.broadcast_to(abi[:, sl], (SUBLANE, ts))
            br, bi = bre_ref[j], bim_ref[j]
            bbar_scr[j, :, 0:ts] = (cr[:, sl] * br - ci[:, sl] * bi).astype(BF16)
            bbar_scr[j, :, ts:2 * ts] = (cr[:, sl] * bi + ci[:, sl] * br).astype(BF16)
            cmat_scr[j, 0:ts, :] = cre_ref[j].astype(BF16)
            cmat_scr[j, ts:2 * ts, :] = (-cim_ref[j]).astype(BF16)
            carry_scr[j, :, 0:ts] = h0r_ref[:, sl]
            carry_scr[j, :, ts:2 * ts] = h0i_ref[:, sl]

    u = u_ref[...].reshape(rows, SSM_W)
    ub = u.astype(BF16)
    for j in range(SSM_LANE_TILES):
        h_scr[j] = jnp.dot(ub[:, LANE * j:LANE * (j + 1)], bbar_scr[j], preferred_element_type=F32)
    ys = []
    for j in range(SSM_LANE_TILES):
        for q in range(ts // cw):
            re_sl = slice(q * cw, (q + 1) * cw)
            im_sl = slice(ts + q * cw, ts + (q + 1) * cw)
            ar = jnp.broadcast_to(a_scr[j, 0:1, re_sl], (B, cw))
            ai = jnp.broadcast_to(a_scr[j, 0:1, im_sl], (B, cw))
            hr, hi = carry_scr[j, :, re_sl], carry_scr[j, :, im_sl]
            for t in range(Tb):
                rs = slice(t * B, (t + 1) * B)
                hr, hi = (ar * hr - ai * hi + h_scr[j, rs, re_sl],
                          ar * hi + ai * hr + h_scr[j, rs, im_sl])
                h_scr[j, rs, re_sl] = hr
                h_scr[j, rs, im_sl] = hi
            carry_scr[j, :, re_sl] = hr
            carry_scr[j, :, im_sl] = hi
        ys.append(jnp.dot(h_scr[j].astype(BF16), cmat_scr[j], preferred_element_type=F32))
    y = jnp.concatenate(ys, axis=-1) + d_ref[...] * u
    g = jax.nn.gelu(y)
    gl = jnp.dot(g.astype(BF16), wglu_ref[...], preferred_element_type=F32) + bglu_ref[...]
    out = gl[:, :SSM_W] * jax.nn.sigmoid(gl[:, SSM_W:])
    y_ref[...] = out.reshape(Tb, B, SSM_W)

    @pl.when(step == pl.num_programs(0) - 1)
    def _():
        for j in range(SSM_LANE_TILES):
            sl = slice(ts * j, ts * (j + 1))
            htr_ref[:, sl] = carry_scr[j, :, 0:ts]
            hti_ref[:, sl] = carry_scr[j, :, ts:2 * ts]


def _ssm_call(u_tm, h0r, h0i, sp, wglu_bf16, bglu, *, Tb):
    L, B, _ = u_tm.shape
    rows = Tb * B
    cw = max(LANE, min(SSM_TILE_STATES, 4096 // B))
    kern = functools.partial(_ssm_kernel, B=B, Tb=Tb, cw=cw)
    full = lambda shape: pl.BlockSpec(shape, lambda i: (0,) * len(shape))
    nt, ts = SSM_LANE_TILES, SSM_TILE_STATES
    return pl.pallas_call(
        kern,
        grid=(L // Tb,),
        in_specs=[
            pl.BlockSpec((Tb, B, SSM_W), lambda i: (i, 0, 0)),
            full((B, N_STATES)), full((B, N_STATES)),
            full((1, N_STATES)), full((1, N_STATES)), full((1, N_STATES)),
            full((nt, LANE, ts)), full((nt, LANE, ts)),
            full((nt, ts, LANE)), full((nt, ts, LANE)),
            full((1, SSM_W)), full((SSM_W, 2 * SSM_W)), full((1, 2 * SSM_W)),
        ],
        out_specs=[
            pl.BlockSpec((Tb, B, SSM_W), lambda i: (i, 0, 0)),
            full((B, N_STATES)), full((B, N_STATES)),
        ],
        out_shape=[
            jax.ShapeDtypeStruct((L, B, SSM_W), F32),
            jax.ShapeDtypeStruct((B, N_STATES), F32),
            jax.ShapeDtypeStruct((B, N_STATES), F32),
        ],
        scratch_shapes=[
            pltpu.VMEM((nt, SUBLANE, 2 * ts), F32),
            pltpu.VMEM((nt, LANE, 2 * ts), BF16),
            pltpu.VMEM((nt, 2 * ts, LANE), BF16),
            pltpu.VMEM((nt, rows, 2 * ts), F32),
            pltpu.VMEM((nt, B, 2 * ts), F32),
        ],
        compiler_params=_params(("arbitrary",)),
        name="s5_scan",
    )(u_tm, h0r, h0i, sp["a_re"], sp["a_im"], sp["log_dt"], sp["b_re"], sp["b_im"],
      sp["c_re"], sp["c_im"], sp["d"], wglu_bf16, bglu)


def _ssm_layouts(a_re, a_im, log_dt, b_re, b_im, c_re, c_im, d_skip):
    gpt = SSM_GROUPS // SSM_LANE_TILES
    eye = jnp.eye(gpt, dtype=bool)[None, :, None, :, None]

    def blockdiag(m, rows, cols):
        return jnp.where(eye, m[:, :, :, None, :], 0.0).reshape(SSM_LANE_TILES, rows, cols)

    def b_tiles(b):
        bt = jnp.transpose(b, (0, 2, 1)).reshape(SSM_LANE_TILES, gpt, SSM_GC, SSM_STATE)
        return blockdiag(bt, LANE, SSM_TILE_STATES)

    def c_tiles(c):
        ct = jnp.transpose(c, (0, 2, 1)).reshape(SSM_LANE_TILES, gpt, SSM_STATE, SSM_GC)
        return blockdiag(ct, SSM_TILE_STATES, LANE)

    return {
        "a_re": a_re.reshape(1, N_STATES), "a_im": a_im.reshape(1, N_STATES),
        "log_dt": jnp.repeat(log_dt, SSM_STATE).reshape(1, N_STATES),
        "b_re": b_tiles(b_re), "b_im": b_tiles(b_im),
        "c_re": c_tiles(c_re), "c_im": c_tiles(c_im),
        "d": d_skip.reshape(1, SSM_W),
    }


def _pool_kernel(u_ref, halo_ref, hist_ref, w_ref, sc_ref, o_ref, *, tl, start_pos):
    i = pl.program_id(1)
    u = u_ref[0]
    halo = jnp.where(i == 0, hist_ref[0], halo_ref[0])
    ext = jnp.concatenate([halo, u], axis=0)
    sums = []
    s = ext
    for k in (1, 2, 4, 8):
        s = s + pltpu.roll(s, k, 0)
        sums.append(s)
    pos = start_pos + i * tl + lax.broadcasted_iota(jnp.int32, (tl, 1), 0)
    outs = []
    for g, w in enumerate(POOL_WINDOWS):
        ls = slice(g * POOL_GROUP_W, (g + 1) * POOL_GROUP_W)
        cnt = jnp.minimum(w, pos + 1).astype(F32)
        diff = sums[g][POOL_HALO:, ls] / cnt - u[:, ls]
        outs.append(jnp.dot(diff.astype(BF16), w_ref[g], preferred_element_type=F32))
    o_ref[0] = (jnp.concatenate(outs, axis=-1) * sc_ref[...]).astype(BF16)


def _pool_call(u_pool, hist, w_bf16, scale, *, tl, start_pos):
    B, L, _ = u_pool.shape
    hpb = tl // POOL_HALO
    kern = functools.partial(_pool_kernel, tl=tl, start_pos=start_pos)
    return pl.pallas_call(
        kern,
        grid=(B, L // tl),
        in_specs=[
            pl.BlockSpec((1, tl, POOL_W), lambda b, i: (b, i, 0)),
            pl.BlockSpec((1, POOL_HALO, POOL_W), lambda b, i: (b, jnp.maximum(i * hpb - 1, 0), 0)),
            pl.BlockSpec((1, POOL_HALO, POOL_W), lambda b, i: (b, 0, 0)),
            pl.BlockSpec((len(POOL_WINDOWS), POOL_GROUP_W, POOL_GROUP_W), lambda b, i: (0, 0, 0)),
            pl.BlockSpec((1, POOL_W), lambda b, i: (0, 0)),
        ],
        out_specs=pl.BlockSpec((1, tl, POOL_W), lambda b, i: (b, i, 0)),
        out_shape=jax.ShapeDtypeStruct((B, L, POOL_W), BF16),
        compiler_params=_params(("arbitrary", "arbitrary")),
        name="pool_mix",
    )(u_pool, u_pool, hist, w_bf16, scale)


def _gated_norm(y, z, g):
    y = y.astype(F32)
    ms = jnp.mean(y * y, axis=-1, keepdims=True)
    return (y * lax.rsqrt(ms + EPS) * g * jax.nn.silu(z.astype(F32))).astype(BF16)


def _outproj_kernel(x_ref, ya_ref, ys_ref, yp_ref, z_ref, g_ref, gate_ref, w_ref, fg_ref, o_ref,
                    *, bb, tl, final):
    tm = bb * tl
    g = g_ref[...]
    z = z_ref[...].reshape(tm, D_MODEL)
    ya = ya_ref[...].reshape(tm, ATT_W)
    ys_tm = ys_ref[...]
    ys = jnp.concatenate([ys_tm[:, b * SSM_W:(b + 1) * SSM_W] for b in range(bb)], axis=0)
    yp = yp_ref[...].reshape(tm, POOL_W)
    a_att = _gated_norm(ya, z[:, 0:ATT_W], g[:, 0:ATT_W])
    a_ssm = _gated_norm(ys, z[:, ATT_W:ATT_W + SSM_W], g[:, ATT_W:ATT_W + SSM_W])
    a_pool = _gated_norm(yp, z[:, ATT_W + SSM_W:], g[:, ATT_W + SSM_W:])
    acc = jnp.dot(a_att, w_ref[0:ATT_W, :], preferred_element_type=F32)
    acc = acc + jnp.dot(a_ssm, w_ref[ATT_W:ATT_W + SSM_W, :], preferred_element_type=F32)
    acc = acc + jnp.dot(a_pool, w_ref[ATT_W + SSM_W:, :], preferred_element_type=F32)
    x_new = x_ref[...] + gate_ref[...] * acc.reshape(bb, tl, D_MODEL)
    if final:
        ms = jnp.mean(x_new * x_new, axis=-1, keepdims=True)
        x_new = x_new * lax.rsqrt(ms + EPS) * fg_ref[...]
    o_ref[...] = x_new


def _outproj_call(x, y_att, y_ssm_tm, y_pool, z, g, gate, w_bf16, final_g, *, bb, tl, final):
    B, L, _ = x.shape
    kern = functools.partial(_outproj_kernel, bb=bb, tl=tl, final=final)
    rows = lambda w: pl.BlockSpec((bb, tl, w), lambda b, i: (b, i, 0))
    return pl.pallas_call(
        kern,
        grid=(B // bb, L // tl),
        in_specs=[
            rows(D_MODEL), rows(ATT_W),
            pl.BlockSpec((tl, bb * SSM_W), lambda b, i: (i, b)),
            rows(POOL_W), rows(D_MODEL),
            _resident((1, D_MODEL)),
            pl.BlockSpec((bb, 1, D_MODEL), lambda b, i: (b, 0, 0)),
            _resident((D_MODEL, D_MODEL)),
            _resident((1, D_MODEL)),
        ],
        out_specs=rows(D_MODEL),
        out_shape=jax.ShapeDtypeStruct((B, L, D_MODEL), F32),
        compiler_params=_params(("arbitrary", "arbitrary")),
        name="out_proj",
    )(x, y_att, y_ssm_tm, y_pool, z, g, gate, w_bf16, final_g)


def _layer(x, mod, attend, h0r, h0i, pool_hist, lw, *, start_pos, keep, cfg, final):
    B, L, _ = x.shape
    shift, scale, gate = (mod[:, None, k * D_MODEL:(k + 1) * D_MODEL] for k in range(3))
    qkv, z, u_pool, u_tm, kv = _inproj_call(x, shift, scale, lw["norm_g"], lw["w_in"],
                                            bb=cfg["in_bb"], tl=cfg["in_tl"], keep=keep)
    y_att = attend(qkv)
    y_ssm_tm, htr, hti = _ssm_call(u_tm.reshape(L, B, SSM_W), h0r, h0i, lw["ssm"], lw["w_glu"], lw["b_glu"],
                                   Tb=cfg["ssm_tb"])
    y_pool = _pool_call(u_pool, pool_hist, lw["w_pool"], lw["pool_scale"], tl=cfg["pool_tl"], start_pos=start_pos)
    x_new = _outproj_call(x, y_att, y_ssm_tm.reshape(L, B * SSM_W), y_pool, z, lw["branch_g"], gate,
                          lw["w_out"], lw["final_g"], bb=cfg["out_bb"], tl=cfg["out_tl"], final=final)
    return x_new, kv, u_pool, htr, hti


def kernel(x_prompt, x_sample, c_prompt, c_sample, cache_k, cache_v, state_ssm_re, state_ssm_im, state_pool, norm_g, w_ada, b_ada, w_in, rel_bias, ssm_a_re, ssm_a_im, ssm_log_dt, ssm_b_re, ssm_b_im, ssm_c_re, ssm_c_im, ssm_d, w_glu, b_glu, w_pool, pool_scale, branch_norm_g, w_out, final_norm_g):
    depth = w_in.shape[0]
    Bp, Lp, _ = x_prompt.shape
    Bs, Ls, _ = x_sample.shape
    hist_s = cache_k.shape[2]
    assert Lp % PREV_ROWS == 0 and Ls == CHUNK and hist_s == PREV_ROWS and Ls >= POOL_HIST
    keep = min(PREV_ROWS, Lp)

    cfg_p = dict(in_bb=1, in_tl=256, tq=PREV_ROWS, sq=256, ssm_tb=512 // Bp, pool_tl=512, out_bb=1, out_tl=256)
    cfg_s = dict(in_bb=256 // Ls, in_tl=Ls, tq=Ls, ssm_tb=512 // Bs, pool_tl=Ls, out_bb=256 // Ls, out_tl=Ls)

    mod = _ada_call(jnp.concatenate([c_prompt, c_sample], axis=0), w_ada, b_ada)
    bias_p = _attn_bias(rel_bias, cfg_p["sq"])
    bias_s = _attn_bias(rel_bias, Ls)
    kt_cache = jnp.transpose(cache_k, (0, 1, 3, 4, 2))
    vt_cache = jnp.transpose(cache_v, (0, 1, 3, 4, 2))
    w_in_b, w_out_b, w_glu_b, w_pool_b = (w.astype(BF16) for w in (w_in, w_out, w_glu, w_pool))

    xp, xs = x_prompt, x_sample
    zeros_state = jnp.zeros((Bp, N_STATES), F32)
    zeros_pool = jnp.zeros((Bp, POOL_HALO, POOL_W), F32)
    outs = [[] for _ in range(10)]
    for l in range(depth):
        lw = {
            "norm_g": norm_g[l].reshape(1, D_MODEL),
            "w_in": w_in_b[l],
            "ssm": _ssm_layouts(ssm_a_re[l], ssm_a_im[l], ssm_log_dt[l], ssm_b_re[l], ssm_b_im[l],
                                ssm_c_re[l], ssm_c_im[l], ssm_d[l]),
            "w_glu": w_glu_b[l],
            "b_glu": b_glu[l].reshape(1, 2 * SSM_W),
            "w_pool": w_pool_b[l],
            "pool_scale": pool_scale[l].reshape(1, POOL_W),
            "branch_g": branch_norm_g[l].reshape(1, D_MODEL),
            "w_out": w_out_b[l],
            "final_g": final_norm_g.reshape(1, D_MODEL),
        }
        final = l == depth - 1
        attend_p = functools.partial(_attn_call, bias=bias_p[l], tq=cfg_p["tq"], sq=cfg_p["sq"])
        xp, kv_p, up_p, htr_p, hti_p = _layer(xp, mod[l, :Bp], attend_p, zeros_state, zeros_state, zeros_pool, lw,
                                              start_pos=0, keep=keep, cfg=cfg_p, final=final)
        attend_s = functools.partial(_attn_cached_call, kt_cache=kt_cache, vt_cache=vt_cache, layer=l,
                                     bias=bias_s[l], tq=cfg_s["tq"])
        pool_hist_s = jnp.pad(state_pool[l], ((0, 0), (POOL_HALO - POOL_HIST, 0), (0, 0)))
        xs, kv_s, up_s, htr_s, hti_s = _layer(
            xs, mod[l, Bp:], attend_s, state_ssm_re[l].reshape(Bs, N_STATES), state_ssm_im[l].reshape(Bs, N_STATES),
            pool_hist_s, lw, start_pos=PAST_LEN, keep=Ls, cfg=cfg_s, final=final)

        heads = lambda a: a.reshape(a.shape[0], a.shape[1], N_HEADS, HEAD_DIM)
        state = lambda a: a.reshape(a.shape[0], SSM_GROUPS, SSM_STATE)
        outs[0].append(heads(kv_p[:, :, :ATT_W]))
        outs[1].append(heads(kv_p[:, :, ATT_W:]))
        outs[2].append(state(htr_p))
        outs[3].append(state(hti_p))
        outs[4].append(up_p[:, Lp - POOL_HIST:])
        outs[5].append(heads(kv_s[:, :, :ATT_W]))
        outs[6].append(heads(kv_s[:, :, ATT_W:]))
        outs[7].append(state(htr_s))
        outs[8].append(state(hti_s))
        outs[9].append(up_s[:, Ls - POOL_HIST:])

    return (xp, xs) + tuple(jnp.stack(o) for o in outs)
```

```python
import functools

import jax
import jax.numpy as jnp
from jax import lax
from jax.experimental import pallas as pl
from jax.experimental.pallas import tpu as pltpu

F32 = jnp.float32
BF16 = jnp.bfloat16

D_MODEL = 2048
ATT_W = 1024
SSM_W = 512
POOL_W = 512
N_HEADS = 16
HEAD_DIM = 64
CHUNK = 64
PREV_ROWS = 512
BAND_CHUNKS = PREV_ROWS // CHUNK + 1
REL_CLIP = 256
PAST_LEN = 1024
SSM_GROUPS = 32
SSM_STATE = 64
SSM_GC = 16
N_STATES = SSM_GROUPS * SSM_STATE
POOL_WINDOWS = (2, 4, 8, 16)
POOL_GROUP_W = 128
POOL_HALO = 16
POOL_HIST = 15
IN_COLS = 6144
COL_Q, COL_K, COL_V, COL_ZATT = 0, 1024, 2048, 3072
COL_USSM, COL_ZSSM, COL_UPOOL, COL_ZPOOL = 4096, 4608, 5120, 5632
EPS = 1e-6
NEG_INF = -1e30
LOG2E = 1.4426950408889634

LANE = 128
SUBLANE = 8
SSM_LANE_TILES = SSM_W // LANE
SSM_TILE_STATES = N_STATES // SSM_LANE_TILES
VMEM_LIMIT = 56 * 1024 * 1024


def _params(sem):
    return pltpu.CompilerParams(dimension_semantics=sem, vmem_limit_bytes=VMEM_LIMIT)


def _resident(shape):
    return pl.BlockSpec(shape, lambda *_: (0,) * len(shape), pipeline_mode=pl.Buffered(1))


def _of_layer(shape, layer):
    return pl.BlockSpec((None,) + tuple(shape), lambda *_: (layer,) + (0,) * len(shape),
                        pipeline_mode=pl.Buffered(1))


def _mod_spec(which, bb, row0, layer):
    return pl.BlockSpec((None, None, bb, 1, D_MODEL), lambda b, i: (layer, which, row0 // bb + b, 0, 0))


def _ada_kernel(c_ref, w_ref, b_ref, o_ref):
    a = jax.nn.silu(c_ref[...]).astype(BF16)
    o_ref[0] = jnp.dot(a, w_ref[0].astype(BF16), preferred_element_type=F32) + b_ref[0]


def _ada_call(c_all, w_ada, b_ada):
    depth = w_ada.shape[0]
    rows = c_all.shape[0]
    tn = 768
    return pl.pallas_call(
        _ada_kernel,
        grid=(depth, 3 * D_MODEL // tn),
        in_specs=[
            pl.BlockSpec((rows, D_MODEL), lambda l, j: (0, 0)),
            pl.BlockSpec((1, D_MODEL, tn), lambda l, j: (l, 0, j)),
            pl.BlockSpec((1, 1, tn), lambda l, j: (l, 0, j)),
        ],
        out_specs=pl.BlockSpec((1, rows, tn), lambda l, j: (l, 0, j)),
        out_shape=jax.ShapeDtypeStruct((depth, rows, 3 * D_MODEL), F32),
        compiler_params=_params(("arbitrary", "arbitrary")),
        name="ada_mod",
    )(c_all, w_ada, b_ada.reshape(depth, 1, 3 * D_MODEL))


def _inproj_kernel(x_ref, sh_ref, sc_ref, g_ref, w_ref, qkv_ref, z_ref, up_ref, utm_ref, kv_ref, h_scr,
                   *, bb, tl, kv_first):
    i = pl.program_id(1)
    g = g_ref[...]
    lane_tiles = [slice(k * LANE, (k + 1) * LANE) for k in range(D_MODEL // LANE)]
    for b in range(bb):
        acc = None
        for sl in lane_tiles:
            xk = x_ref[b, :, sl]
            acc = xk * xk if acc is None else acc + xk * xk
        ms = jnp.sum(acc, axis=-1, keepdims=True) * (1.0 / D_MODEL)
        rs = lax.rsqrt(ms + EPS)
        gs = g * (1.0 + sc_ref[b])
        sh = sh_ref[b]
        for sl in lane_tiles:
            h_scr[b * tl:(b + 1) * tl, sl] = (x_ref[b, :, sl] * rs * gs[:, sl] + sh[:, sl]).astype(BF16)

    h = h_scr[...]

    def proj(c0, width):
        return jnp.dot(h, w_ref[:, c0:c0 + width], preferred_element_type=F32)

    def rows3(a):
        return a.reshape(bb, tl, a.shape[-1])

    qkv_ref[:, :, 0:ATT_W] = rows3((proj(COL_Q, ATT_W) * (HEAD_DIM ** -0.5 * LOG2E)).astype(BF16))
    k = proj(COL_K, ATT_W)
    qkv_ref[:, :, ATT_W:2 * ATT_W] = rows3(k.astype(BF16))
    v = proj(COL_V, ATT_W)
    qkv_ref[:, :, 2 * ATT_W:3 * ATT_W] = rows3(v.astype(BF16))

    @pl.when(i >= kv_first)
    def _():
        kv_ref[:, :, 0:ATT_W] = rows3(k)
        kv_ref[:, :, ATT_W:2 * ATT_W] = rows3(v)

    z_ref[:, :, 0:ATT_W] = rows3(proj(COL_ZATT, ATT_W).astype(BF16))
    z_ref[:, :, ATT_W:ATT_W + SSM_W] = rows3(proj(COL_ZSSM, SSM_W).astype(BF16))
    z_ref[:, :, ATT_W + SSM_W:] = rows3(proj(COL_ZPOOL, POOL_W).astype(BF16))
    up_ref[...] = rows3(proj(COL_UPOOL, POOL_W))
    us = proj(COL_USSM, SSM_W)
    for b in range(bb):
        utm_ref[:, b * SSM_W:(b + 1) * SSM_W] = us[b * tl:(b + 1) * tl, :]


def _inproj_call(x, mod, row0, g_all, w_all, layer, *, bb, tl, keep):
    B, L, _ = x.shape
    kv_first = (L - keep) // tl
    kern = functools.partial(_inproj_kernel, bb=bb, tl=tl, kv_first=kv_first)
    rows = lambda w: pl.BlockSpec((bb, tl, w), lambda b, i: (b, i, 0))
    return pl.pallas_call(
        kern,
        grid=(B // bb, L // tl),
        in_specs=[
            rows(D_MODEL),
            _mod_spec(0, bb, row0, layer),
            _mod_spec(1, bb, row0, layer),
            _of_layer((1, D_MODEL), layer),
            _of_layer((D_MODEL, IN_COLS), layer),
        ],
        out_specs=[
            rows(3 * ATT_W), rows(D_MODEL), rows(POOL_W),
            pl.BlockSpec((tl, bb * SSM_W), lambda b, i: (i, b)),
            pl.BlockSpec((bb, tl, 2 * ATT_W), lambda b, i: (b, jnp.maximum(i - kv_first, 0), 0)),
        ],
        out_shape=[
            jax.ShapeDtypeStruct((B, L, 3 * ATT_W), BF16),
            jax.ShapeDtypeStruct((B, L, D_MODEL), BF16),
            jax.ShapeDtypeStruct((B, L, POOL_W), F32),
            jax.ShapeDtypeStruct((L, B * SSM_W), F32),
            jax.ShapeDtypeStruct((B, keep, 2 * ATT_W), F32),
        ],
        scratch_shapes=[pltpu.VMEM((bb * tl, D_MODEL), BF16)],
        compiler_params=_params(("arbitrary", "arbitrary")),
        name="in_proj",
    )(x, mod, mod, g_all, w_all)


def _softmax_pv(scores, values):
    m = scores[0].max(axis=-1, keepdims=True)
    for sc in scores[1:]:
        m = jnp.maximum(m, sc.max(axis=-1, keepdims=True))
    den = None
    acc = None
    for sc, (val, transposed) in zip(scores, values):
        e = jnp.exp2(sc - m)
        d = jnp.sum(e, axis=-1, keepdims=True)
        dims = (((1,), (1,)), ((), ())) if transposed else (((1,), (0,)), ((), ()))
        o = lax.dot_general(e.astype(BF16), val, dims, preferred_element_type=F32)
        den = d if den is None else den + d
        acc = o if acc is None else acc + o
    return acc / den


_NT = (((1,), (1,)), ((), ()))


def _attn_kernel(q_ref, kp_ref, kc_ref, vp_ref, vc_ref, bias_ref, o_ref, b2, *, tq, sq, pairs):
    i = pl.program_id(2)
    lane = lax.broadcasted_iota(jnp.int32, (1, LANE), 1)
    own = [lane < HEAD_DIM, lane >= HEAD_DIM]
    hpp = LANE // HEAD_DIM
    n_sub = tq // sq

    def run(first_block):
        def pieces_of(s):
            pieces = [(kc_ref, vc_ref, 0, (s + 1) * sq, PREV_ROWS - s * sq)]
            if not first_block:
                pieces.insert(0, (kp_ref, vp_ref, s * sq, PREV_ROWS, 0))
            return pieces

        def scores_of(s, p, hh):
            ls = slice(p * LANE, (p + 1) * LANE)
            q = q_ref[0, s * sq:(s + 1) * sq, ls]
            qh = jnp.where(own[hh], q, jnp.zeros_like(q))
            return [lax.dot_general(qh, kr[0, a:b, ls], _NT, preferred_element_type=F32)
                    + b2[p * hpp + hh, :, c0:c0 + (b - a)] for kr, _, a, b, c0 in pieces_of(s)]

        def weighted_values(s, p, hh, scores):
            ls = slice(p * LANE, (p + 1) * LANE)
            m = scores[0].max(axis=-1, keepdims=True)
            for sc in scores[1:]:
                m = jnp.maximum(m, sc.max(axis=-1, keepdims=True))
            o = None
            for sc, (_, vr, a, b, _) in zip(scores, pieces_of(s)):
                v = vr[0, a:b, ls]
                vh = jnp.where(own[hh], v, jnp.ones_like(v))
                pv = jnp.dot(jnp.exp2(sc - m).astype(BF16), vh, preferred_element_type=F32)
                o = pv if o is None else o + pv
            return o

        chains = [(s, p, hh) for s in range(n_sub) for p in range(pairs) for hh in range(hpp)]
        scores, acc = {chains[0]: scores_of(*chains[0])}, {}
        for n, c in enumerate(chains):
            if n + 1 < len(chains):
                scores[chains[n + 1]] = scores_of(*chains[n + 1])
            acc[c] = weighted_values(*c, scores.pop(c))
        for s in range(n_sub):
            for p in range(pairs):
                a0, a1 = acc[(s, p, 0)], acc[(s, p, 1)]
                num = jnp.where(own[0], a0, a1)
                den = pltpu.roll(jnp.where(own[0], a1, a0), HEAD_DIM, 1)
                o_ref[0, s * sq:(s + 1) * sq, p * LANE:(p + 1) * LANE] = (num / den).astype(BF16)

    @pl.when(i == 0)
    def _():
        b2[...] = bias_ref[...] * LOG2E
        run(True)

    @pl.when(i > 0)
    def _():
        run(False)


def _attn_call(qkv, bias, layer, *, tq, sq, pairs):
    B, L, _ = qkv.shape
    assert tq == PREV_ROWS
    width = pairs * LANE
    groups = ATT_W // width
    heads = width // HEAD_DIM
    kern = functools.partial(_attn_kernel, tq=tq, sq=sq, pairs=pairs)
    cur = lambda c: pl.BlockSpec((1, tq, width), lambda b, h, i: (b, i, c * groups + h))
    prev = lambda c: pl.BlockSpec((1, tq, width), lambda b, h, i: (b, jnp.maximum(i - 1, 0), c * groups + h))
    return pl.pallas_call(
        kern,
        grid=(B, groups, L // tq),
        in_specs=[
            cur(0), prev(1), cur(1), prev(2), cur(2),
            pl.BlockSpec((None, heads, sq, PREV_ROWS + sq), lambda b, h, i: (layer, h, 0, 0)),
        ],
        out_specs=pl.BlockSpec((1, tq, width), lambda b, h, i: (b, i, h)),
        out_shape=jax.ShapeDtypeStruct((B, L, ATT_W), BF16),
        scratch_shapes=[pltpu.VMEM((heads, sq, PREV_ROWS + sq), F32)],
        compiler_params=_params(("arbitrary", "arbitrary", "arbitrary")),
        name="band_attn",
    )(qkv, qkv, qkv, qkv, qkv, bias)


def _attn_cached_kernel(qkv_ref, kt_ref, vt_ref, bias_ref, o_ref, *, tq):
    def scores_of(h):
        hs = slice(h * HEAD_DIM, (h + 1) * HEAD_DIM)
        q = qkv_ref[0, :, hs]
        k_new = qkv_ref[0, :, ATT_W + h * HEAD_DIM:ATT_W + (h + 1) * HEAD_DIM]
        kt = kt_ref[0, 0, h].astype(BF16)
        s_old = jnp.dot(q, kt, preferred_element_type=F32) + bias_ref[h, :, 0:PREV_ROWS] * LOG2E
        s_new = (lax.dot_general(q, k_new, _NT, preferred_element_type=F32)
                 + bias_ref[h, :, PREV_ROWS:] * LOG2E)
        return [s_old, s_new]

    def weighted_values(h, scores):
        v_new = qkv_ref[0, :, 2 * ATT_W + h * HEAD_DIM:2 * ATT_W + (h + 1) * HEAD_DIM]
        vt = vt_ref[0, 0, h].astype(BF16)
        return _softmax_pv(scores, [(vt, True), (v_new, False)])

    ahead = 4
    scores = {h: scores_of(h) for h in range(ahead)}
    outs = []
    for h in range(N_HEADS):
        if h + ahead < N_HEADS:
            scores[h + ahead] = scores_of(h + ahead)
        outs.append(weighted_values(h, scores.pop(h)))
    o_ref[0] = jnp.concatenate(outs, axis=-1).astype(BF16)


def _attn_cached_call(qkv, kt_cache, vt_cache, layer, bias, *, tq):
    B, L, _ = qkv.shape
    assert L == tq == CHUNK
    kern = functools.partial(_attn_cached_kernel, tq=tq)
    cache_spec = pl.BlockSpec((1, 1, N_HEADS, HEAD_DIM, PREV_ROWS), lambda b: (layer, b, 0, 0, 0))
    return pl.pallas_call(
        kern,
        grid=(B,),
        in_specs=[
            pl.BlockSpec((1, tq, 3 * ATT_W), lambda b: (b, 0, 0)),
            cache_spec, cache_spec,
            _of_layer((N_HEADS, tq, PREV_ROWS + tq), layer),
        ],
        out_specs=pl.BlockSpec((1, tq, ATT_W), lambda b: (b, 0, 0)),
        out_shape=jax.ShapeDtypeStruct((B, L, ATT_W), BF16),
        compiler_params=_params(("arbitrary",)),
        name="cached_attn",
    )(qkv, kt_cache, vt_cache, bias)


def _attn_bias(rel_bias, sq):
    depth, H, _ = rel_bias.shape
    wl = PREV_ROWS + sq
    period = wl + sq + 1
    m = jnp.arange(period)
    m = jnp.where(m < wl, m, m - period)
    table = rel_bias[:, :, jnp.clip(PREV_ROWS - m, -REL_CLIP, REL_CLIP) + REL_CLIP]
    skew = jnp.tile(table, (1, 1, sq))[:, :, :sq * (period - 1)].reshape(depth, H, sq, period - 1)[..., :wl]
    i = jnp.arange(sq)[:, None]
    r = jnp.arange(wl)[None, :]
    lo = (i // CHUNK) * CHUNK
    band = (r >= lo) & (r < lo + BAND_CHUNKS * CHUNK)
    return jnp.where(band, skew, NEG_INF)


def _ssm_kernel(u_ref, h0r_ref, h0i_ref, are_ref, aim_ref, ldt_ref, bre_ref, bim_ref, cre_ref, cim_ref,
                d_ref, wglu_ref, bglu_ref, y_ref, htr_ref, hti_ref,
                a_scr, bbar_scr, cmat_scr, h_scr, carry_scr, *, B, Tb, cw):
    step = pl.program_id(0)
    rows = Tb * B
    ts = SSM_TILE_STATES

    @pl.when(step == 0)
    def _():
        dt = jnp.exp(ldt_ref[...])
        ar, ai = are_ref[...], aim_ref[...]
        mag = jnp.exp(ar * dt)
        ang = ai * dt
        abr, abi = mag * jnp.cos(ang), mag * jnp.sin(ang)
        den = ar * ar + ai * ai
        nr, ni = abr - 1.0, abi
        cr = (nr * ar + ni * ai) / den
        ci = (ni * ar - nr * ai) / den
        for j in range(SSM_LANE_TILES):
            sl = slice(ts * j, ts * (j + 1))
            a_scr[j, :, 0:ts] = jnp.broadcast_to(abr[:, sl], (SUBLANE, ts))
            a_scr[j, :, ts:2 * ts] = jnp.broadcast_to(abi[:, sl], (SUBLANE, ts))
            br, bi = bre_ref[j], bim_ref[j]
            bbar_scr[j, :, 0:ts] = (cr[:, sl] * br - ci[:, sl] * bi).astype(BF16)
            bbar_scr[j, :, ts:2 * ts] = (cr[:, sl] * bi + ci[:, sl] * br).astype(BF16)
            cmat_scr[j, 0:ts, :] = cre_ref[j].astype(BF16)
            cmat_scr[j, ts:2 * ts, :] = (-cim_ref[j]).astype(BF16)
            carry_scr[j, :, 0:ts] = h0r_ref[:, sl]
            carry_scr[j, :, ts:2 * ts] = h0i_ref[:, sl]

    u = u_ref[...].reshape(rows, SSM_W)
    ub = u.astype(BF16)
    for j in range(SSM_LANE_TILES):
        h_scr[j] = jnp.dot(ub[:, LANE * j:LANE * (j + 1)], bbar_scr[j], preferred_element_type=F32)
    ys = []
    for j in range(SSM_LANE_TILES):
        for q in range(ts // cw):
            re_sl = slice(q * cw, (q + 1) * cw)
            im_sl = slice(ts + q * cw, ts + (q + 1) * cw)
            ar = jnp.broadcast_to(a_scr[j, 0:1, re_sl], (B, cw))
            ai = jnp.broadcast_to(a_scr[j, 0:1, im_sl], (B, cw))
            hr, hi = carry_scr[j, :, re_sl], carry_scr[j, :, im_sl]
            for t in range(Tb):
                rs = slice(t * B, (t + 1) * B)
                hr, hi = (ar * hr - ai * hi + h_scr[j, rs, re_sl],
                          ar * hi + ai * hr + h_scr[j, rs, im_sl])
                h_scr[j, rs, re_sl] = hr
                h_scr[j, rs, im_sl] = hi
            carry_scr[j, :, re_sl] = hr
            carry_scr[j, :, im_sl] = hi
        ys.append(jnp.dot(h_scr[j].astype(BF16), cmat_scr[j], preferred_element_type=F32))
    y = jnp.concatenate(ys, axis=-1) + d_ref[...] * u
    g = jax.nn.gelu(y)
    gl = jnp.dot(g.astype(BF16), wglu_ref[...], preferred_element_type=F32) + bglu_ref[...]
    out = gl[:, :SSM_W] * jax.nn.sigmoid(gl[:, SSM_W:])
    y_ref[...] = out.reshape(Tb, B, SSM_W)

    @pl.when(step == pl.num_programs(0) - 1)
    def _():
        for j in range(SSM_LANE_TILES):
            sl = slice(ts * j, ts * (j + 1))
            htr_ref[:, sl] = carry_scr[j, :, 0:ts]
            hti_ref[:, sl] = carry_scr[j, :, ts:2 * ts]


def _ssm_call(u_tm, h0r, h0i, h0_layer, sp, wglu_all, bglu_all, layer, *, Tb):
    L, B, _ = u_tm.shape
    rows = Tb * B
    cw = max(LANE, min(SSM_TILE_STATES, 4096 // B))
    kern = functools.partial(_ssm_kernel, B=B, Tb=Tb, cw=cw)
    full = lambda shape: pl.BlockSpec(shape, lambda i: (0,) * len(shape))
    nt, ts = SSM_LANE_TILES, SSM_TILE_STATES
    return pl.pallas_call(
        kern,
        grid=(L // Tb,),
        in_specs=[
            pl.BlockSpec((Tb, B, SSM_W), lambda i: (i, 0, 0)),
            _of_layer((B, N_STATES), h0_layer), _of_layer((B, N_STATES), h0_layer),
            _of_layer((1, N_STATES), layer), _of_layer((1, N_STATES), layer), _of_layer((1, N_STATES), layer),
            _of_layer((nt, LANE, ts), layer), _of_layer((nt, LANE, ts), layer),
            _of_layer((nt, ts, LANE), layer), _of_layer((nt, ts, LANE), layer),
            _of_layer((1, SSM_W), layer), _of_layer((SSM_W, 2 * SSM_W), layer),
            _of_layer((1, 2 * SSM_W), layer),
        ],
        out_specs=[
            pl.BlockSpec((Tb, B, SSM_W), lambda i: (i, 0, 0)),
            full((B, N_STATES)), full((B, N_STATES)),
        ],
        out_shape=[
            jax.ShapeDtypeStruct((L, B, SSM_W), F32),
            jax.ShapeDtypeStruct((B, N_STATES), F32),
            jax.ShapeDtypeStruct((B, N_STATES), F32),
        ],
        scratch_shapes=[
            pltpu.VMEM((nt, SUBLANE, 2 * ts), F32),
            pltpu.VMEM((nt, LANE, 2 * ts), BF16),
            pltpu.VMEM((nt, 2 * ts, LANE), BF16),
            pltpu.VMEM((nt, rows, 2 * ts), F32),
            pltpu.VMEM((nt, B, 2 * ts), F32),
        ],
        compiler_params=_params(("arbitrary",)),
        name="s5_scan",
    )(u_tm, h0r, h0i, sp["a_re"], sp["a_im"], sp["log_dt"], sp["b_re"], sp["b_im"],
      sp["c_re"], sp["c_im"], sp["d"], wglu_all, bglu_all)


def _ssm_layouts(a_re, a_im, log_dt, b_re, b_im, c_re, c_im, d_skip):
    depth = a_re.shape[0]
    gpt = SSM_GROUPS // SSM_LANE_TILES
    eye = jnp.eye(gpt, dtype=bool)[None, None, :, None, :, None]

    def blockdiag(m, rows, cols):
        return jnp.where(eye, m[:, :, :, :, None, :], 0.0).reshape(depth, SSM_LANE_TILES, rows, cols)

    def b_tiles(b):
        bt = jnp.transpose(b, (0, 1, 3, 2)).reshape(depth, SSM_LANE_TILES, gpt, SSM_GC, SSM_STATE)
        return blockdiag(bt, LANE, SSM_TILE_STATES)

    def c_tiles(c):
        ct = jnp.transpose(c, (0, 1, 3, 2)).reshape(depth, SSM_LANE_TILES, gpt, SSM_STATE, SSM_GC)
        return blockdiag(ct, SSM_TILE_STATES, LANE)

    return {
        "a_re": a_re.reshape(depth, 1, N_STATES), "a_im": a_im.reshape(depth, 1, N_STATES),
        "log_dt": jnp.repeat(log_dt, SSM_STATE, axis=1).reshape(depth, 1, N_STATES),
        "b_re": b_tiles(b_re), "b_im": b_tiles(b_im),
        "c_re": c_tiles(c_re), "c_im": c_tiles(c_im),
        "d": d_skip.reshape(depth, 1, SSM_W),
    }


def _pool_kernel(u_ref, halo_ref, hist_ref, w_ref, sc_ref, o_ref, *, tl, start_pos):
    i = pl.program_id(1)
    u = u_ref[0]
    halo = jnp.where(i == 0, hist_ref[0], halo_ref[0])
    ext = jnp.concatenate([halo, u], axis=0)
    sums = []
    s = ext
    for k in (1, 2, 4, 8):
        s = s + pltpu.roll(s, k, 0)
        sums.append(s)
    pos = start_pos + i * tl + lax.broadcasted_iota(jnp.int32, (tl, 1), 0)
    outs = []
    for g, w in enumerate(POOL_WINDOWS):
        ls = slice(g * POOL_GROUP_W, (g + 1) * POOL_GROUP_W)
        cnt = jnp.minimum(w, pos + 1).astype(F32)
        diff = sums[g][POOL_HALO:, ls] / cnt - u[:, ls]
        outs.append(jnp.dot(diff.astype(BF16), w_ref[g], preferred_element_type=F32))
    o_ref[0] = (jnp.concatenate(outs, axis=-1) * sc_ref[...]).astype(BF16)


def _pool_call(u_pool, hist, hist_layer, w_all, scale_all, layer, *, tl, start_pos):
    B, L, _ = u_pool.shape
    hpb = tl // POOL_HALO
    kern = functools.partial(_pool_kernel, tl=tl, start_pos=start_pos)
    return pl.pallas_call(
        kern,
        grid=(B, L // tl),
        in_specs=[
            pl.BlockSpec((1, tl, POOL_W), lambda b, i: (b, i, 0)),
            pl.BlockSpec((1, POOL_HALO, POOL_W), lambda b, i: (b, jnp.maximum(i * hpb - 1, 0), 0)),
            pl.BlockSpec((None, 1, POOL_HALO, POOL_W), lambda b, i: (hist_layer, b, 0, 0)),
            _of_layer((len(POOL_WINDOWS), POOL_GROUP_W, POOL_GROUP_W), layer),
            _of_layer((1, POOL_W), layer),
        ],
        out_specs=pl.BlockSpec((1, tl, POOL_W), lambda b, i: (b, i, 0)),
        out_shape=jax.ShapeDtypeStruct((B, L, POOL_W), BF16),
        compiler_params=_params(("arbitrary", "arbitrary")),
        name="pool_mix",
    )(u_pool, u_pool, hist, w_all, scale_all)


def _gated_norm(y, z, g):
    y = y.astype(F32)
    ms = jnp.mean(y * y, axis=-1, keepdims=True)
    return (y * lax.rsqrt(ms + EPS) * g * jax.nn.silu(z.astype(F32))).astype(BF16)


def _outproj_kernel(x_ref, ya_ref, ys_ref, yp_ref, z_ref, g_ref, gate_ref, w_ref, fg_ref, o_ref,
                    *, bb, tl, final):
    tm = bb * tl
    g = g_ref[...]
    z = z_ref[...].reshape(tm, D_MODEL)
    ya = ya_ref[...].reshape(tm, ATT_W)
    ys_tm = ys_ref[...]
    ys = jnp.concatenate([ys_tm[:, b * SSM_W:(b + 1) * SSM_W] for b in range(bb)], axis=0)
    yp = yp_ref[...].reshape(tm, POOL_W)
    a_att = _gated_norm(ya, z[:, 0:ATT_W], g[:, 0:ATT_W])
    a_ssm = _gated_norm(ys, z[:, ATT_W:ATT_W + SSM_W], g[:, ATT_W:ATT_W + SSM_W])
    a_pool = _gated_norm(yp, z[:, ATT_W + SSM_W:], g[:, ATT_W + SSM_W:])
    acc = jnp.dot(a_att, w_ref[0:ATT_W, :], preferred_element_type=F32)
    acc = acc + jnp.dot(a_ssm, w_ref[ATT_W:ATT_W + SSM_W, :], preferred_element_type=F32)
    acc = acc + jnp.dot(a_pool, w_ref[ATT_W + SSM_W:, :], preferred_element_type=F32)
    x_new = x_ref[...] + gate_ref[...] * acc.reshape(bb, tl, D_MODEL)
    if final:
        ms = jnp.mean(x_new * x_new, axis=-1, keepdims=True)
        x_new = x_new * lax.rsqrt(ms + EPS) * fg_ref[...]
    o_ref[...] = x_new


def _outproj_call(x, y_att, y_ssm_tm, y_pool, z, g_all, mod, row0, w_all, final_g, layer, *, bb, tl, final):
    B, L, _ = x.shape
    kern = functools.partial(_outproj_kernel, bb=bb, tl=tl, final=final)
    rows = lambda w: pl.BlockSpec((bb, tl, w), lambda b, i: (b, i, 0))
    return pl.pallas_call(
        kern,
        grid=(B // bb, L // tl),
        in_specs=[
            rows(D_MODEL), rows(ATT_W),
            pl.BlockSpec((tl, bb * SSM_W), lambda b, i: (i, b)),
            rows(POOL_W), rows(D_MODEL),
            _of_layer((1, D_MODEL), layer),
            _mod_spec(2, bb, row0, layer),
            _of_layer((D_MODEL, D_MODEL), layer),
            _resident((1, D_MODEL)),
        ],
        out_specs=rows(D_MODEL),
        out_shape=jax.ShapeDtypeStruct((B, L, D_MODEL), F32),
        compiler_params=_params(("arbitrary", "arbitrary")),
        name="out_proj",
    )(x, y_att, y_ssm_tm, y_pool, z, g_all, mod, w_all, final_g)


def _layer(x, layer, row0, attend, h0, pool_hist, pw, *, start_pos, keep, cfg, final):
    B, L, _ = x.shape
    qkv, z, u_pool, u_tm, kv = _inproj_call(x, pw["mod"], row0, pw["norm_g"], pw["w_in"], layer,
                                            bb=cfg["in_bb"], tl=cfg["in_tl"], keep=keep)
    y_att = attend(qkv)
    y_ssm_tm, htr, hti = _ssm_call(u_tm.reshape(L, B, SSM_W), *h0, pw["ssm"], pw["w_glu"], pw["b_glu"], layer,
                                   Tb=cfg["ssm_tb"])
    y_pool = _pool_call(u_pool, *pool_hist, pw["w_pool"], pw["pool_scale"], layer,
                        tl=cfg["pool_tl"], start_pos=start_pos)
    x_new = _outproj_call(x, y_att, y_ssm_tm.reshape(L, B * SSM_W), y_pool, z, pw["branch_g"], pw["mod"], row0,
                          pw["w_out"], pw["final_g"], layer, bb=cfg["out_bb"], tl=cfg["out_tl"], final=final)
    return x_new, kv, u_pool, htr, hti


def kernel(x_prompt, x_sample, c_prompt, c_sample, cache_k, cache_v, state_ssm_re, state_ssm_im, state_pool, norm_g, w_ada, b_ada, w_in, rel_bias, ssm_a_re, ssm_a_im, ssm_log_dt, ssm_b_re, ssm_b_im, ssm_c_re, ssm_c_im, ssm_d, w_glu, b_glu, w_pool, pool_scale, branch_norm_g, w_out, final_norm_g):
    depth = w_in.shape[0]
    Bp, Lp, _ = x_prompt.shape
    Bs, Ls, _ = x_sample.shape
    hist_s = cache_k.shape[2]
    assert Lp % PREV_ROWS == 0 and Ls == CHUNK and hist_s == PREV_ROWS and Ls >= POOL_HIST
    keep = min(PREV_ROWS, Lp)

    cfg_p = dict(in_bb=1, in_tl=256, tq=PREV_ROWS, sq=256, pairs=4, ssm_tb=512 // Bp, pool_tl=512,
                 out_bb=1, out_tl=256)
    cfg_s = dict(in_bb=256 // Ls, in_tl=Ls, tq=Ls, ssm_tb=512 // Bs, pool_tl=Ls, out_bb=256 // Ls, out_tl=Ls)

    n_rows = Bp + Bs
    mod = _ada_call(jnp.concatenate([c_prompt, c_sample], axis=0), w_ada, b_ada)
    bias_p = _attn_bias(rel_bias, cfg_p["sq"])
    bias_s = _attn_bias(rel_bias, Ls)
    kt_cache = jnp.transpose(cache_k, (0, 1, 3, 4, 2))
    vt_cache = jnp.transpose(cache_v, (0, 1, 3, 4, 2))
    pw = {
        "mod": jnp.transpose(mod.reshape(depth, n_rows, 3, D_MODEL), (0, 2, 1, 3))[:, :, :, None, :],
        "norm_g": norm_g.reshape(depth, 1, D_MODEL),
        "w_in": w_in.astype(BF16),
        "ssm": _ssm_layouts(ssm_a_re, ssm_a_im, ssm_log_dt, ssm_b_re, ssm_b_im, ssm_c_re, ssm_c_im, ssm_d),
        "w_glu": w_glu.astype(BF16),
        "b_glu": b_glu.reshape(depth, 1, 2 * SSM_W),
        "w_pool": w_pool.astype(BF16),
        "pool_scale": pool_scale.reshape(depth, 1, POOL_W),
        "branch_g": branch_norm_g.reshape(depth, 1, D_MODEL),
        "w_out": w_out.astype(BF16),
        "final_g": final_norm_g.reshape(1, D_MODEL),
    }
    zeros_state = jnp.zeros((1, Bp, N_STATES), F32)
    zeros_pool = jnp.zeros((1, Bp, POOL_HALO, POOL_W), F32)
    state_re = state_ssm_re.reshape(depth, Bs, N_STATES)
    state_im = state_ssm_im.reshape(depth, Bs, N_STATES)
    pool_hist_s = jnp.pad(state_pool, ((0, 0), (0, 0), (POOL_HALO - POOL_HIST, 0), (0, 0)))

    xp, xs = x_prompt, x_sample
    outs = [[] for _ in range(10)]
    for l in range(depth):
        final = l == depth - 1
        attend_p = functools.partial(_attn_call, bias=bias_p, layer=l, tq=cfg_p["tq"], sq=cfg_p["sq"],
                                     pairs=cfg_p["pairs"])
        xp, kv_p, up_p, htr_p, hti_p = _layer(xp, l, 0, attend_p, (zeros_state, zeros_state, 0), (zeros_pool, 0), pw,
                                              start_pos=0, keep=keep, cfg=cfg_p, final=final)
        attend_s = functools.partial(_attn_cached_call, kt_cache=kt_cache, vt_cache=vt_cache, layer=l,
                                     bias=bias_s, tq=cfg_s["tq"])
        xs, kv_s, up_s, htr_s, hti_s = _layer(xs, l, Bp, attend_s, (state_re, state_im, l), (pool_hist_s, l), pw,
                                              start_pos=PAST_LEN, keep=Ls, cfg=cfg_s, final=final)

        heads = lambda a: a.reshape(a.shape[0], a.shape[1], N_HEADS, HEAD_DIM)
        state = lambda a: a.reshape(a.shape[0], SSM_GROUPS, SSM_STATE)
        outs[0].append(heads(kv_p[:, :, :ATT_W]))
        outs[1].append(heads(kv_p[:, :, ATT_W:]))
        outs[2].append(state(htr_p))
        outs[3].append(state(hti_p))
        outs[4].append(up_p[:, Lp - POOL_HIST:])
        outs[5].append(heads(kv_s[:, :, :ATT_W]))
        outs[6].append(heads(kv_s[:, :, ATT_W:]))
        outs[7].append(state(htr_s))
        outs[8].append(state(hti_s))
        outs[9].append(up_s[:, Ls - POOL_HIST:])

    return (xp, xs) + tuple(jnp.stack(o) for o in outs)
```

```python
import functools

import jax
import jax.numpy as jnp
from jax import lax
from jax.experimental import pallas as pl
from jax.experimental.pallas import tpu as pltpu

F32 = jnp.float32
BF16 = jnp.bfloat16

D_MODEL = 2048
ATT_W = 1024
SSM_W = 512
POOL_W = 512
N_HEADS = 16
HEAD_DIM = 64
CHUNK = 64
PREV_ROWS = 512
BAND_CHUNKS = PREV_ROWS // CHUNK + 1
REL_CLIP = 256
PAST_LEN = 1024
SSM_GROUPS = 32
SSM_STATE = 64
SSM_GC = 16
N_STATES = SSM_GROUPS * SSM_STATE
POOL_WINDOWS = (2, 4, 8, 16)
POOL_GROUP_W = 128
POOL_HALO = 16
POOL_HIST = 15
IN_COLS = 6144
COL_Q, COL_K, COL_V, COL_ZATT = 0, 1024, 2048, 3072
COL_USSM, COL_ZSSM, COL_UPOOL, COL_ZPOOL = 4096, 4608, 5120, 5632
EPS = 1e-6
NEG_INF = -1e30
LOG2E = 1.4426950408889634

LANE = 128
SUBLANE = 8
SSM_LANE_TILES = SSM_W // LANE
SSM_TILE_STATES = N_STATES // SSM_LANE_TILES
VMEM_LIMIT = 56 * 1024 * 1024


def _params(sem):
    return pltpu.CompilerParams(dimension_semantics=sem, vmem_limit_bytes=VMEM_LIMIT)


def _resident(shape):
    return pl.BlockSpec(shape, lambda *_: (0,) * len(shape), pipeline_mode=pl.Buffered(1))


def _of_layer(shape, layer):
    return pl.BlockSpec((None,) + tuple(shape), lambda *_: (layer,) + (0,) * len(shape),
                        pipeline_mode=pl.Buffered(1))


def _mod_spec(which, bb, row0, layer):
    return pl.BlockSpec((None, None, bb, 1, D_MODEL), lambda b, i: (layer, which, row0 // bb + b, 0, 0))


def _ada_kernel(c_ref, w_ref, b_ref, o_ref):
    a = jax.nn.silu(c_ref[...]).astype(BF16)
    o_ref[0] = jnp.dot(a, w_ref[0].astype(BF16), preferred_element_type=F32) + b_ref[0]


def _ada_call(c_all, w_ada, b_ada):
    depth = w_ada.shape[0]
    rows = c_all.shape[0]
    tn = 768
    return pl.pallas_call(
        _ada_kernel,
        grid=(depth, 3 * D_MODEL // tn),
        in_specs=[
            pl.BlockSpec((rows, D_MODEL), lambda l, j: (0, 0)),
            pl.BlockSpec((1, D_MODEL, tn), lambda l, j: (l, 0, j)),
            pl.BlockSpec((1, 1, tn), lambda l, j: (l, 0, j)),
        ],
        out_specs=pl.BlockSpec((1, rows, tn), lambda l, j: (l, 0, j)),
        out_shape=jax.ShapeDtypeStruct((depth, rows, 3 * D_MODEL), F32),
        compiler_params=_params(("arbitrary", "arbitrary")),
        name="ada_mod",
    )(c_all, w_ada, b_ada.reshape(depth, 1, 3 * D_MODEL))


def _inproj_kernel(x_ref, sh_ref, sc_ref, g_ref, w_ref, qkv_ref, z_ref, up_ref, utm_ref, kv_ref, h_scr,
                   *, bb, tl, kv_first):
    i = pl.program_id(1)
    g = g_ref[...]
    lane_tiles = [slice(k * LANE, (k + 1) * LANE) for k in range(D_MODEL // LANE)]
    for b in range(bb):
        acc = None
        for sl in lane_tiles:
            xk = x_ref[b, :, sl]
            acc = xk * xk if acc is None else acc + xk * xk
        ms = jnp.sum(acc, axis=-1, keepdims=True) * (1.0 / D_MODEL)
        rs = lax.rsqrt(ms + EPS)
        gs = g * (1.0 + sc_ref[b])
        sh = sh_ref[b]
        for sl in lane_tiles:
            h_scr[b * tl:(b + 1) * tl, sl] = (x_ref[b, :, sl] * rs * gs[:, sl] + sh[:, sl]).astype(BF16)

    h = h_scr[...]

    def proj(c0, width):
        return jnp.dot(h, w_ref[:, c0:c0 + width], preferred_element_type=F32)

    def rows3(a):
        return a.reshape(bb, tl, a.shape[-1])

    qkv_ref[:, :, 0:ATT_W] = rows3((proj(COL_Q, ATT_W) * (HEAD_DIM ** -0.5 * LOG2E)).astype(BF16))
    k = proj(COL_K, ATT_W)
    qkv_ref[:, :, ATT_W:2 * ATT_W] = rows3(k.astype(BF16))
    v = proj(COL_V, ATT_W)
    qkv_ref[:, :, 2 * ATT_W:3 * ATT_W] = rows3(v.astype(BF16))

    @pl.when(i >= kv_first)
    def _():
        kv_ref[:, :, 0:ATT_W] = rows3(k)
        kv_ref[:, :, ATT_W:2 * ATT_W] = rows3(v)

    z_ref[:, :, 0:ATT_W] = rows3(proj(COL_ZATT, ATT_W).astype(BF16))
    z_ref[:, :, ATT_W:ATT_W + SSM_W] = rows3(proj(COL_ZSSM, SSM_W).astype(BF16))
    z_ref[:, :, ATT_W + SSM_W:] = rows3(proj(COL_ZPOOL, POOL_W).astype(BF16))
    up_ref[...] = rows3(proj(COL_UPOOL, POOL_W))
    us = proj(COL_USSM, SSM_W)
    for b in range(bb):
        utm_ref[:, b * SSM_W:(b + 1) * SSM_W] = us[b * tl:(b + 1) * tl, :]


def _inproj_call(x, mod, row0, g_all, w_all, layer, *, bb, tl, keep):
    B, L, _ = x.shape
    kv_first = (L - keep) // tl
    kern = functools.partial(_inproj_kernel, bb=bb, tl=tl, kv_first=kv_first)
    rows = lambda w: pl.BlockSpec((bb, tl, w), lambda b, i: (b, i, 0))
    return pl.pallas_call(
        kern,
        grid=(B // bb, L // tl),
        in_specs=[
            rows(D_MODEL),
            _mod_spec(0, bb, row0, layer),
            _mod_spec(1, bb, row0, layer),
            _of_layer((1, D_MODEL), layer),
            _of_layer((D_MODEL, IN_COLS), layer),
        ],
        out_specs=[
            rows(3 * ATT_W), rows(D_MODEL), rows(POOL_W),
            pl.BlockSpec((tl, bb * SSM_W), lambda b, i: (i, b)),
            pl.BlockSpec((bb, tl, 2 * ATT_W), lambda b, i: (b, jnp.maximum(i - kv_first, 0), 0)),
        ],
        out_shape=[
            jax.ShapeDtypeStruct((B, L, 3 * ATT_W), BF16),
            jax.ShapeDtypeStruct((B, L, D_MODEL), BF16),
            jax.ShapeDtypeStruct((B, L, POOL_W), F32),
            jax.ShapeDtypeStruct((L, B * SSM_W), F32),
            jax.ShapeDtypeStruct((B, keep, 2 * ATT_W), F32),
        ],
        scratch_shapes=[pltpu.VMEM((bb * tl, D_MODEL), BF16)],
        compiler_params=_params(("arbitrary", "arbitrary")),
        name="in_proj",
    )(x, mod, mod, g_all, w_all)


def _softmax_pv(scores, values):
    m = scores[0].max(axis=-1, keepdims=True)
    for sc in scores[1:]:
        m = jnp.maximum(m, sc.max(axis=-1, keepdims=True))
    den = None
    acc = None
    for sc, (val, transposed) in zip(scores, values):
        e = jnp.exp2(sc - m)
        d = jnp.sum(e, axis=-1, keepdims=True)
        dims = (((1,), (1,)), ((), ())) if transposed else (((1,), (0,)), ((), ()))
        o = lax.dot_general(e.astype(BF16), val, dims, preferred_element_type=F32)
        den = d if den is None else den + d
        acc = o if acc is None else acc + o
    return acc / den


_NT = (((1,), (1,)), ((), ()))


def _attn_kernel(q_ref, kp_ref, kc_ref, vp_ref, vc_ref, bias_ref, o_ref, b2, *, tq, sq, pairs):
    i = pl.program_id(2)
    lane = lax.broadcasted_iota(jnp.int32, (1, LANE), 1)
    own = [lane < HEAD_DIM, lane >= HEAD_DIM]
    hpp = LANE // HEAD_DIM
    n_sub = tq // sq

    def run(first_block):
        def pieces_of(s):
            pieces = [(kc_ref, vc_ref, 0, (s + 1) * sq, PREV_ROWS - s * sq)]
            if not first_block:
                pieces.insert(0, (kp_ref, vp_ref, s * sq, PREV_ROWS, 0))
            return pieces

        def scores_of(s, p, hh):
            ls = slice(p * LANE, (p + 1) * LANE)
            q = q_ref[0, s * sq:(s + 1) * sq, ls]
            qh = jnp.where(own[hh], q, jnp.zeros_like(q))
            return [lax.dot_general(qh, kr[0, a:b, ls], _NT, preferred_element_type=F32)
                    + b2[p * hpp + hh, :, c0:c0 + (b - a)] for kr, _, a, b, c0 in pieces_of(s)]

        def weighted_values(s, p, hh, scores):
            ls = slice(p * LANE, (p + 1) * LANE)
            m = scores[0].max(axis=-1, keepdims=True)
            for sc in scores[1:]:
                m = jnp.maximum(m, sc.max(axis=-1, keepdims=True))
            o = None
            for sc, (_, vr, a, b, _) in zip(scores, pieces_of(s)):
                v = vr[0, a:b, ls]
                vh = jnp.where(own[hh], v, jnp.ones_like(v))
                pv = jnp.dot(jnp.exp2(sc - m).astype(BF16), vh, preferred_element_type=F32)
                o = pv if o is None else o + pv
            return o

        chains = [(s, p, hh) for s in range(n_sub) for p in range(pairs) for hh in range(hpp)]
        scores, acc = {chains[0]: scores_of(*chains[0])}, {}
        for n, c in enumerate(chains):
            if n + 1 < len(chains):
                scores[chains[n + 1]] = scores_of(*chains[n + 1])
            acc[c] = weighted_values(*c, scores.pop(c))
        for s in range(n_sub):
            for p in range(pairs):
                a0, a1 = acc[(s, p, 0)], acc[(s, p, 1)]
                num = jnp.where(own[0], a0, a1)
                den = pltpu.roll(jnp.where(own[0], a1, a0), HEAD_DIM, 1)
                o_ref[0, s * sq:(s + 1) * sq, p * LANE:(p + 1) * LANE] = (num / den).astype(BF16)

    @pl.when(i == 0)
    def _():
        b2[...] = bias_ref[...] * LOG2E
        run(True)

    @pl.when(i > 0)
    def _():
        run(False)


def _attn_call(qkv, bias, layer, *, tq, sq, pairs):
    B, L, _ = qkv.shape
    assert tq == PREV_ROWS
    width = pairs * LANE
    groups = ATT_W // width
    heads = width // HEAD_DIM
    kern = functools.partial(_attn_kernel, tq=tq, sq=sq, pairs=pairs)
    cur = lambda c: pl.BlockSpec((1, tq, width), lambda b, h, i: (b, i, c * groups + h))
    prev = lambda c: pl.BlockSpec((1, tq, width), lambda b, h, i: (b, jnp.maximum(i - 1, 0), c * groups + h))
    return pl.pallas_call(
        kern,
        grid=(B, groups, L // tq),
        in_specs=[
            cur(0), prev(1), cur(1), prev(2), cur(2),
            pl.BlockSpec((None, heads, sq, PREV_ROWS + sq), lambda b, h, i: (layer, h, 0, 0)),
        ],
        out_specs=pl.BlockSpec((1, tq, width), lambda b, h, i: (b, i, h)),
        out_shape=jax.ShapeDtypeStruct((B, L, ATT_W), BF16),
        scratch_shapes=[pltpu.VMEM((heads, sq, PREV_ROWS + sq), F32)],
        compiler_params=_params(("arbitrary", "arbitrary", "arbitrary")),
        name="band_attn",
    )(qkv, qkv, qkv, qkv, qkv, bias)


def _attn_cached_kernel(qkv_ref, kt_ref, vt_ref, bias_ref, o_ref, *, tq):
    def scores_of(h):
        hs = slice(h * HEAD_DIM, (h + 1) * HEAD_DIM)
        q = qkv_ref[0, :, hs]
        k_new = qkv_ref[0, :, ATT_W + h * HEAD_DIM:ATT_W + (h + 1) * HEAD_DIM]
        kt = kt_ref[0, 0, h].astype(BF16)
        s_old = jnp.dot(q, kt, preferred_element_type=F32) + bias_ref[h, :, 0:PREV_ROWS] * LOG2E
        s_new = (lax.dot_general(q, k_new, _NT, preferred_element_type=F32)
                 + bias_ref[h, :, PREV_ROWS:] * LOG2E)
        return [s_old, s_new]

    def weighted_values(h, scores):
        v_new = qkv_ref[0, :, 2 * ATT_W + h * HEAD_DIM:2 * ATT_W + (h + 1) * HEAD_DIM]
        vt = vt_ref[0, 0, h].astype(BF16)
        return _softmax_pv(scores, [(vt, True), (v_new, False)])

    ahead = 4
    scores = {h: scores_of(h) for h in range(ahead)}
    outs = []
    for h in range(N_HEADS):
        if h + ahead < N_HEADS:
            scores[h + ahead] = scores_of(h + ahead)
        outs.append(weighted_values(h, scores.pop(h)))
    o_ref[0] = jnp.concatenate(outs, axis=-1).astype(BF16)


def _attn_cached_call(qkv, kt_cache, vt_cache, layer, bias, *, tq):
    B, L, _ = qkv.shape
    assert L == tq == CHUNK
    kern = functools.partial(_attn_cached_kernel, tq=tq)
    cache_spec = pl.BlockSpec((1, 1, N_HEADS, HEAD_DIM, PREV_ROWS), lambda b: (layer, b, 0, 0, 0))
    return pl.pallas_call(
        kern,
        grid=(B,),
        in_specs=[
            pl.BlockSpec((1, tq, 3 * ATT_W), lambda b: (b, 0, 0)),
            cache_spec, cache_spec,
            _of_layer((N_HEADS, tq, PREV_ROWS + tq), layer),
        ],
        out_specs=pl.BlockSpec((1, tq, ATT_W), lambda b: (b, 0, 0)),
        out_shape=jax.ShapeDtypeStruct((B, L, ATT_W), BF16),
        compiler_params=_params(("arbitrary",)),
        name="cached_attn",
    )(qkv, kt_cache, vt_cache, bias)


def _attn_bias(rel_bias, sq):
    depth, H, _ = rel_bias.shape
    wl = PREV_ROWS + sq
    period = wl + sq + 1
    m = jnp.arange(period)
    m = jnp.where(m < wl, m, m - period)
    table = rel_bias[:, :, jnp.clip(PREV_ROWS - m, -REL_CLIP, REL_CLIP) + REL_CLIP]
    skew = jnp.tile(table, (1, 1, sq))[:, :, :sq * (period - 1)].reshape(depth, H, sq, period - 1)[..., :wl]
    i = jnp.arange(sq)[:, None]
    r = jnp.arange(wl)[None, :]
    lo = (i // CHUNK) * CHUNK
    band = (r >= lo) & (r < lo + BAND_CHUNKS * CHUNK)
    return jnp.where(band, skew, NEG_INF)


def _ssm_kernel(u_ref, h0r_ref, h0i_ref, are_ref, aim_ref, ldt_ref, bre_ref, bim_ref, cre_ref, cim_ref,
                d_ref, wglu_ref, bglu_ref, y_ref, htr_ref, hti_ref,
                a_scr, bbar_scr, cmat_scr, h_scr, carry_scr, *, B, Tb, nsub, cw):
    step = pl.program_id(0)
    rows = Tb * B
    ts = SSM_TILE_STATES

    @pl.when(step == 0)
    def _():
        dt = jnp.exp(ldt_ref[...])
        ar, ai = are_ref[...], aim_ref[...]
        mag = jnp.exp(ar * dt)
        ang = ai * dt
        abr, abi = mag * jnp.cos(ang), mag * jnp.sin(ang)
        den = ar * ar + ai * ai
        nr, ni = abr - 1.0, abi
        cr = (nr * ar + ni * ai) / den
        ci = (ni * ar - nr * ai) / den
        for j in range(SSM_LANE_TILES):
            sl = slice(ts * j, ts * (j + 1))
            a_scr[j, :, 0:ts] = jnp.broadcast_to(abr[:, sl], (SUBLANE, ts))
            a_scr[j, :, ts:2 * ts] = jnp.broadcast_to(abi[:, sl], (SUBLANE, ts))
            br, bi = bre_ref[j], bim_ref[j]
            bbar_scr[j, :, 0:ts] = (cr[:, sl] * br - ci[:, sl] * bi).astype(BF16)
            bbar_scr[j, :, ts:2 * ts] = (cr[:, sl] * bi + ci[:, sl] * br).astype(BF16)
            cmat_scr[j, 0:ts, :] = cre_ref[j].astype(BF16)
            cmat_scr[j, ts:2 * ts, :] = (-cim_ref[j]).astype(BF16)
            carry_scr[j, :, 0:ts] = h0r_ref[:, sl]
            carry_scr[j, :, ts:2 * ts] = h0i_ref[:, sl]

    tsub = Tb // nsub
    srows = tsub * B
    u = u_ref[...].reshape(rows, SSM_W)
    ub = u.astype(BF16)
    for sb in range(nsub):
        for j in range(SSM_LANE_TILES):
            h_scr[sb, j] = jnp.dot(ub[sb * srows:(sb + 1) * srows, LANE * j:LANE * (j + 1)], bbar_scr[j],
                                   preferred_element_type=F32)
    for sb in range(nsub):
        ys = []
        for j in range(SSM_LANE_TILES):
            for q in range(ts // cw):
                re_sl = slice(q * cw, (q + 1) * cw)
                im_sl = slice(ts + q * cw, ts + (q + 1) * cw)
                ar = jnp.broadcast_to(a_scr[j, 0:1, re_sl], (B, cw))
                ai = jnp.broadcast_to(a_scr[j, 0:1, im_sl], (B, cw))
                hr, hi = carry_scr[j, :, re_sl], carry_scr[j, :, im_sl]
                for t in range(tsub):
                    rs = slice(t * B, (t + 1) * B)
                    hr, hi = (ar * hr - ai * hi + h_scr[sb, j, rs, re_sl],
                              ar * hi + ai * hr + h_scr[sb, j, rs, im_sl])
                    h_scr[sb, j, rs, re_sl] = hr
                    h_scr[sb, j, rs, im_sl] = hi
                carry_scr[j, :, re_sl] = hr
                carry_scr[j, :, im_sl] = hi
            ys.append(jnp.dot(h_scr[sb, j].astype(BF16), cmat_scr[j], preferred_element_type=F32))
        y = jnp.concatenate(ys, axis=-1) + d_ref[...] * u[sb * srows:(sb + 1) * srows]
        g = jax.nn.gelu(y)
        gl = jnp.dot(g.astype(BF16), wglu_ref[...], preferred_element_type=F32) + bglu_ref[...]
        out = gl[:, :SSM_W] * jax.nn.sigmoid(gl[:, SSM_W:])
        y_ref[sb * tsub:(sb + 1) * tsub] = out.reshape(tsub, B, SSM_W)

    @pl.when(step == pl.num_programs(0) - 1)
    def _():
        for j in range(SSM_LANE_TILES):
            sl = slice(ts * j, ts * (j + 1))
            htr_ref[:, sl] = carry_scr[j, :, 0:ts]
            hti_ref[:, sl] = carry_scr[j, :, ts:2 * ts]


def _ssm_call(u_tm, h0r, h0i, h0_layer, sp, wglu_all, bglu_all, layer, *, Tb, nsub):
    L, B, _ = u_tm.shape
    rows = Tb * B
    cw = max(LANE, min(SSM_TILE_STATES, 4096 // B))
    kern = functools.partial(_ssm_kernel, B=B, Tb=Tb, nsub=nsub, cw=cw)
    full = lambda shape: pl.BlockSpec(shape, lambda i: (0,) * len(shape))
    nt, ts = SSM_LANE_TILES, SSM_TILE_STATES
    return pl.pallas_call(
        kern,
        grid=(L // Tb,),
        in_specs=[
            pl.BlockSpec((Tb, B, SSM_W), lambda i: (i, 0, 0)),
            _of_layer((B, N_STATES), h0_layer), _of_layer((B, N_STATES), h0_layer),
            _of_layer((1, N_STATES), layer), _of_layer((1, N_STATES), layer), _of_layer((1, N_STATES), layer),
            _of_layer((nt, LANE, ts), layer), _of_layer((nt, LANE, ts), layer),
            _of_layer((nt, ts, LANE), layer), _of_layer((nt, ts, LANE), layer),
            _of_layer((1, SSM_W), layer), _of_layer((SSM_W, 2 * SSM_W), layer),
            _of_layer((1, 2 * SSM_W), layer),
        ],
        out_specs=[
            pl.BlockSpec((Tb, B, SSM_W), lambda i: (i, 0, 0)),
            full((B, N_STATES)), full((B, N_STATES)),
        ],
        out_shape=[
            jax.ShapeDtypeStruct((L, B, SSM_W), F32),
            jax.ShapeDtypeStruct((B, N_STATES), F32),
            jax.ShapeDtypeStruct((B, N_STATES), F32),
        ],
        scratch_shapes=[
            pltpu.VMEM((nt, SUBLANE, 2 * ts), F32),
            pltpu.VMEM((nt, LANE, 2 * ts), BF16),
            pltpu.VMEM((nt, 2 * ts, LANE), BF16),
            pltpu.VMEM((nsub, nt, rows // nsub, 2 * ts), F32),
            pltpu.VMEM((nt, B, 2 * ts), F32),
        ],
        compiler_params=_params(("arbitrary",)),
        name="s5_scan",
    )(u_tm, h0r, h0i, sp["a_re"], sp["a_im"], sp["log_dt"], sp["b_re"], sp["b_im"],
      sp["c_re"], sp["c_im"], sp["d"], wglu_all, bglu_all)


def _ssm_layouts(a_re, a_im, log_dt, b_re, b_im, c_re, c_im, d_skip):
    depth = a_re.shape[0]
    gpt = SSM_GROUPS // SSM_LANE_TILES
    eye = jnp.eye(gpt, dtype=bool)[None, None, :, None, :, None]

    def blockdiag(m, rows, cols):
        return jnp.where(eye, m[:, :, :, :, None, :], 0.0).reshape(depth, SSM_LANE_TILES, rows, cols)

    def b_tiles(b):
        bt = jnp.transpose(b, (0, 1, 3, 2)).reshape(depth, SSM_LANE_TILES, gpt, SSM_GC, SSM_STATE)
        return blockdiag(bt, LANE, SSM_TILE_STATES)

    def c_tiles(c):
        ct = jnp.transpose(c, (0, 1, 3, 2)).reshape(depth, SSM_LANE_TILES, gpt, SSM_STATE, SSM_GC)
        return blockdiag(ct, SSM_TILE_STATES, LANE)

    return {
        "a_re": a_re.reshape(depth, 1, N_STATES), "a_im": a_im.reshape(depth, 1, N_STATES),
        "log_dt": jnp.repeat(log_dt, SSM_STATE, axis=1).reshape(depth, 1, N_STATES),
        "b_re": b_tiles(b_re), "b_im": b_tiles(b_im),
        "c_re": c_tiles(c_re), "c_im": c_tiles(c_im),
        "d": d_skip.reshape(depth, 1, SSM_W),
    }


def _pool_mix(u, halo, first_pos, w_ref, scale):
    tl = u.shape[0]
    ext = jnp.concatenate([halo, u], axis=0)
    sums = []
    s = ext
    for k in (1, 2, 4, 8):
        s = s + pltpu.roll(s, k, 0)
        sums.append(s)
    pos = first_pos + lax.broadcasted_iota(jnp.int32, (tl, 1), 0)
    outs = []
    for g, w in enumerate(POOL_WINDOWS):
        ls = slice(g * POOL_GROUP_W, (g + 1) * POOL_GROUP_W)
        cnt = jnp.minimum(w, pos + 1).astype(F32)
        diff = sums[g][POOL_HALO:, ls] / cnt - u[:, ls]
        outs.append(jnp.dot(diff.astype(BF16), w_ref[g], preferred_element_type=F32))
    return jnp.concatenate(outs, axis=-1) * scale


def _gated_norm(y, z, g):
    y = y.astype(F32)
    ms = jnp.mean(y * y, axis=-1, keepdims=True)
    return (y * lax.rsqrt(ms + EPS) * g * jax.nn.silu(z.astype(F32))).astype(BF16)


def _outproj_kernel(x_ref, ya_ref, ys_ref, up_ref, halo_ref, hist_ref, wp_ref, ps_ref, z_ref, g_ref, gate_ref,
                    w_ref, fg_ref, o_ref, *, bb, tl, final, start_pos):
    i = pl.program_id(1)
    tm = bb * tl
    g = g_ref[...]
    z = z_ref[...].reshape(tm, D_MODEL)
    ya = ya_ref[...].reshape(tm, ATT_W)
    ys_tm = ys_ref[...]
    ys = jnp.concatenate([ys_tm[:, b * SSM_W:(b + 1) * SSM_W] for b in range(bb)], axis=0)
    yp = jnp.concatenate(
        [_pool_mix(up_ref[b], jnp.where(i == 0, hist_ref[b], halo_ref[b]), start_pos + i * tl, wp_ref, ps_ref[...])
         for b in range(bb)], axis=0)
    a_att = _gated_norm(ya, z[:, 0:ATT_W], g[:, 0:ATT_W])
    a_ssm = _gated_norm(ys, z[:, ATT_W:ATT_W + SSM_W], g[:, ATT_W:ATT_W + SSM_W])
    a_pool = _gated_norm(yp, z[:, ATT_W + SSM_W:], g[:, ATT_W + SSM_W:])
    acc = jnp.dot(a_att, w_ref[0:ATT_W, :], preferred_element_type=F32)
    acc = acc + jnp.dot(a_ssm, w_ref[ATT_W:ATT_W + SSM_W, :], preferred_element_type=F32)
    acc = acc + jnp.dot(a_pool, w_ref[ATT_W + SSM_W:, :], preferred_element_type=F32)
    x_new = x_ref[...] + gate_ref[...] * acc.reshape(bb, tl, D_MODEL)
    if final:
        ms = jnp.mean(x_new * x_new, axis=-1, keepdims=True)
        x_new = x_new * lax.rsqrt(ms + EPS) * fg_ref[...]
    o_ref[...] = x_new


def _outproj_call(x, y_att, y_ssm_tm, u_pool, pool_hist, hist_layer, wp_all, ps_all, z, g_all, mod, row0, w_all,
                  final_g, layer, *, bb, tl, final, start_pos):
    B, L, _ = x.shape
    kern = functools.partial(_outproj_kernel, bb=bb, tl=tl, final=final, start_pos=start_pos)
    rows = lambda w: pl.BlockSpec((bb, tl, w), lambda b, i: (b, i, 0))
    hpb = tl // POOL_HALO
    return pl.pallas_call(
        kern,
        grid=(B // bb, L // tl),
        in_specs=[
            rows(D_MODEL), rows(ATT_W),
            pl.BlockSpec((tl, bb * SSM_W), lambda b, i: (i, b)),
            rows(POOL_W),
            pl.BlockSpec((bb, POOL_HALO, POOL_W), lambda b, i: (b, jnp.maximum(i * hpb - 1, 0), 0)),
            pl.BlockSpec((None, bb, POOL_HALO, POOL_W), lambda b, i: (hist_layer, b, 0, 0)),
            _of_layer((len(POOL_WINDOWS), POOL_GROUP_W, POOL_GROUP_W), layer),
            _of_layer((1, POOL_W), layer),
            rows(D_MODEL),
            _of_layer((1, D_MODEL), layer),
            _mod_spec(2, bb, row0, layer),
            _of_layer((D_MODEL, D_MODEL), layer),
            _resident((1, D_MODEL)),
        ],
        out_specs=rows(D_MODEL),
        out_shape=jax.ShapeDtypeStruct((B, L, D_MODEL), F32),
        compiler_params=_params(("arbitrary", "arbitrary")),
        name="out_proj",
    )(x, y_att, y_ssm_tm, u_pool, u_pool, pool_hist, wp_all, ps_all, z, g_all, mod, w_all, final_g)


def _layer(x, layer, row0, attend, h0, pool_hist, pw, *, start_pos, keep, cfg, final):
    B, L, _ = x.shape
    qkv, z, u_pool, u_tm, kv = _inproj_call(x, pw["mod"], row0, pw["norm_g"], pw["w_in"], layer,
                                            bb=cfg["in_bb"], tl=cfg["in_tl"], keep=keep)
    y_att = attend(qkv)
    y_ssm_tm, htr, hti = _ssm_call(u_tm.reshape(L, B, SSM_W), *h0, pw["ssm"], pw["w_glu"], pw["b_glu"], layer,
                                   Tb=cfg["ssm_tb"], nsub=cfg["ssm_nsub"])
    x_new = _outproj_call(x, y_att, y_ssm_tm.reshape(L, B * SSM_W), u_pool, *pool_hist, pw["w_pool"],
                          pw["pool_scale"], z, pw["branch_g"], pw["mod"], row0, pw["w_out"], pw["final_g"], layer,
                          bb=cfg["out_bb"], tl=cfg["out_tl"], final=final, start_pos=start_pos)
    return x_new, kv, u_pool, htr, hti


def kernel(x_prompt, x_sample, c_prompt, c_sample, cache_k, cache_v, state_ssm_re, state_ssm_im, state_pool, norm_g, w_ada, b_ada, w_in, rel_bias, ssm_a_re, ssm_a_im, ssm_log_dt, ssm_b_re, ssm_b_im, ssm_c_re, ssm_c_im, ssm_d, w_glu, b_glu, w_pool, pool_scale, branch_norm_g, w_out, final_norm_g):
    depth = w_in.shape[0]
    Bp, Lp, _ = x_prompt.shape
    Bs, Ls, _ = x_sample.shape
    hist_s = cache_k.shape[2]
    assert Lp % PREV_ROWS == 0 and Ls == CHUNK and hist_s == PREV_ROWS and Ls >= POOL_HIST
    keep = min(PREV_ROWS, Lp)

    cfg_p = dict(in_bb=1, in_tl=256, tq=PREV_ROWS, sq=256, pairs=4, ssm_tb=min(Lp, 1024 // Bp), ssm_nsub=2,
                 out_bb=1, out_tl=512)
    cfg_s = dict(in_bb=256 // Ls, in_tl=Ls, tq=Ls, ssm_tb=min(Ls, 1024 // Bs), ssm_nsub=2,
                 out_bb=256 // Ls, out_tl=Ls)

    n_rows = Bp + Bs
    mod = _ada_call(jnp.concatenate([c_prompt, c_sample], axis=0), w_ada, b_ada)
    bias_p = _attn_bias(rel_bias, cfg_p["sq"])
    bias_s = _attn_bias(rel_bias, Ls)
    kt_cache = jnp.transpose(cache_k, (0, 1, 3, 4, 2))
    vt_cache = jnp.transpose(cache_v, (0, 1, 3, 4, 2))
    pw = {
        "mod": jnp.transpose(mod.reshape(depth, n_rows, 3, D_MODEL), (0, 2, 1, 3))[:, :, :, None, :],
        "norm_g": norm_g.reshape(depth, 1, D_MODEL),
        "w_in": w_in.astype(BF16),
        "ssm": _ssm_layouts(ssm_a_re, ssm_a_im, ssm_log_dt, ssm_b_re, ssm_b_im, ssm_c_re, ssm_c_im, ssm_d),
        "w_glu": w_glu.astype(BF16),
        "b_glu": b_glu.reshape(depth, 1, 2 * SSM_W),
        "w_pool": w_pool.astype(BF16),
        "pool_scale": pool_scale.reshape(depth, 1, POOL_W),
        "branch_g": branch_norm_g.reshape(depth, 1, D_MODEL),
        "w_out": w_out.astype(BF16),
        "final_g": final_norm_g.reshape(1, D_MODEL),
    }
    zeros_state = jnp.zeros((1, Bp, N_STATES), F32)
    zeros_pool = jnp.zeros((1, Bp, POOL_HALO, POOL_W), F32)
    state_re = state_ssm_re.reshape(depth, Bs, N_STATES)
    state_im = state_ssm_im.reshape(depth, Bs, N_STATES)
    pool_hist_s = jnp.pad(state_pool, ((0, 0), (0, 0), (POOL_HALO - POOL_HIST, 0), (0, 0)))

    xp, xs = x_prompt, x_sample
    outs = [[] for _ in range(10)]
    for l in range(depth):
        final = l == depth - 1
        attend_p = functools.partial(_attn_call, bias=bias_p, layer=l, tq=cfg_p["tq"], sq=cfg_p["sq"],
                                     pairs=cfg_p["pairs"])
        xp, kv_p, up_p, htr_p, hti_p = _layer(xp, l, 0, attend_p, (zeros_state, zeros_state, 0), (zeros_pool, 0), pw,
                                              start_pos=0, keep=keep, cfg=cfg_p, final=final)
        attend_s = functools.partial(_attn_cached_call, kt_cache=kt_cache, vt_cache=vt_cache, layer=l,
                                     bias=bias_s, tq=cfg_s["tq"])
        xs, kv_s, up_s, htr_s, hti_s = _layer(xs, l, Bp, attend_s, (state_re, state_im, l), (pool_hist_s, l), pw,
                                              start_pos=PAST_LEN, keep=Ls, cfg=cfg_s, final=final)

        heads = lambda a: a.reshape(a.shape[0], a.shape[1], N_HEADS, HEAD_DIM)
        state = lambda a: a.reshape(a.shape[0], SSM_GROUPS, SSM_STATE)
        outs[0].append(heads(kv_p[:, :, :ATT_W]))
        outs[1].append(heads(kv_p[:, :, ATT_W:]))
        outs[2].append(state(htr_p))
        outs[3].append(state(hti_p))
        outs[4].append(up_p[:, Lp - POOL_HIST:])
        outs[5].append(heads(kv_s[:, :, :ATT_W]))
        outs[6].append(heads(kv_s[:, :, ATT_W:]))
        outs[7].append(state(htr_s))
        outs[8].append(state(hti_s))
        outs[9].append(up_s[:, Ls - POOL_HIST:])

    return (xp, xs) + tuple(jnp.stack(o) for o in outs)
```

```python
import functools

import jax
import jax.numpy as jnp
from jax import lax
from jax.experimental import pallas as pl
from jax.experimental.pallas import tpu as pltpu

F32 = jnp.float32
BF16 = jnp.bfloat16

D_MODEL = 2048
ATT_W = 1024
SSM_W = 512
POOL_W = 512
N_HEADS = 16
HEAD_DIM = 64
CHUNK = 64
PREV_ROWS = 512
BAND_CHUNKS = PREV_ROWS // CHUNK + 1
REL_CLIP = 256
PAST_LEN = 1024
SSM_GROUPS = 32
SSM_STATE = 64
SSM_GC = 16
N_STATES = SSM_GROUPS * SSM_STATE
POOL_WINDOWS = (2, 4, 8, 16)
POOL_GROUP_W = 128
POOL_HALO = 16
POOL_HIST = 15
IN_COLS = 6144
COL_Q, COL_K, COL_V, COL_ZATT = 0, 1024, 2048, 3072
COL_USSM, COL_ZSSM, COL_UPOOL, COL_ZPOOL = 4096, 4608, 5120, 5632
EPS = 1e-6
NEG_INF = -1e30
LOG2E = 1.4426950408889634

LANE = 128
SUBLANE = 8
SSM_LANE_TILES = SSM_W // LANE
SSM_TILE_STATES = N_STATES // SSM_LANE_TILES
VMEM_LIMIT = 56 * 1024 * 1024


def _params(sem):
    return pltpu.CompilerParams(dimension_semantics=sem, vmem_limit_bytes=VMEM_LIMIT)


def _resident(shape):
    return pl.BlockSpec(shape, lambda *_: (0,) * len(shape), pipeline_mode=pl.Buffered(1))


def _of_layer(shape, layer):
    return pl.BlockSpec((None,) + tuple(shape), lambda *_: (layer,) + (0,) * len(shape),
                        pipeline_mode=pl.Buffered(1))


def _mod_spec(which, bb, row0, layer):
    return pl.BlockSpec((None, None, bb, 1, D_MODEL), lambda b, i: (layer, which, row0 // bb + b, 0, 0))


def _ada_kernel(c_ref, w_ref, b_ref, o_ref):
    a = jax.nn.silu(c_ref[...]).astype(BF16)
    o_ref[0] = jnp.dot(a, w_ref[0].astype(BF16), preferred_element_type=F32) + b_ref[0]


def _ada_call(c_all, w_ada, b_ada):
    depth = w_ada.shape[0]
    rows = c_all.shape[0]
    tn = 768
    return pl.pallas_call(
        _ada_kernel,
        grid=(depth, 3 * D_MODEL // tn),
        in_specs=[
            pl.BlockSpec((rows, D_MODEL), lambda l, j: (0, 0)),
            pl.BlockSpec((1, D_MODEL, tn), lambda l, j: (l, 0, j)),
            pl.BlockSpec((1, 1, tn), lambda l, j: (l, 0, j)),
        ],
        out_specs=pl.BlockSpec((1, rows, tn), lambda l, j: (l, 0, j)),
        out_shape=jax.ShapeDtypeStruct((depth, rows, 3 * D_MODEL), F32),
        compiler_params=_params(("arbitrary", "arbitrary")),
        name="ada_mod",
    )(c_all, w_ada, b_ada.reshape(depth, 1, 3 * D_MODEL))


def _inproj_kernel(x_ref, sh_ref, sc_ref, g_ref, w_ref, qkv_ref, z_ref, up_ref, utm_ref, kv_ref, h_scr,
                   *, bb, tl, kv_first):
    i = pl.program_id(1)
    g = g_ref[...]
    lane_tiles = [slice(k * LANE, (k + 1) * LANE) for k in range(D_MODEL // LANE)]
    for b in range(bb):
        acc = None
        for sl in lane_tiles:
            xk = x_ref[b, :, sl]
            acc = xk * xk if acc is None else acc + xk * xk
        ms = jnp.sum(acc, axis=-1, keepdims=True) * (1.0 / D_MODEL)
        rs = lax.rsqrt(ms + EPS)
        gs = g * (1.0 + sc_ref[b])
        sh = sh_ref[b]
        for sl in lane_tiles:
            h_scr[b * tl:(b + 1) * tl, sl] = (x_ref[b, :, sl] * rs * gs[:, sl] + sh[:, sl]).astype(BF16)

    h = h_scr[...]

    def proj(c0, width):
        return jnp.dot(h, w_ref[:, c0:c0 + width], preferred_element_type=F32)

    def rows3(a):
        return a.reshape(bb, tl, a.shape[-1])

    qkv_ref[:, :, 0:ATT_W] = rows3((proj(COL_Q, ATT_W) * (HEAD_DIM ** -0.5 * LOG2E)).astype(BF16))
    k = proj(COL_K, ATT_W)
    qkv_ref[:, :, ATT_W:2 * ATT_W] = rows3(k.astype(BF16))
    v = proj(COL_V, ATT_W)
    qkv_ref[:, :, 2 * ATT_W:3 * ATT_W] = rows3(v.astype(BF16))

    @pl.when(i >= kv_first)
    def _():
        kv_ref[:, :, 0:ATT_W] = rows3(k)
        kv_ref[:, :, ATT_W:2 * ATT_W] = rows3(v)

    z_ref[:, :, 0:ATT_W] = rows3(proj(COL_ZATT, ATT_W).astype(BF16))
    z_ref[:, :, ATT_W:ATT_W + SSM_W] = rows3(proj(COL_ZSSM, SSM_W).astype(BF16))
    z_ref[:, :, ATT_W + SSM_W:] = rows3(proj(COL_ZPOOL, POOL_W).astype(BF16))
    up_ref[...] = rows3(proj(COL_UPOOL, POOL_W))
    us = proj(COL_USSM, SSM_W)
    for b in range(bb):
        utm_ref[:, b * SSM_W:(b + 1) * SSM_W] = us[b * tl:(b + 1) * tl, :]


def _inproj_call(x, mod, row0, g_all, w_all, layer, *, bb, tl, keep):
    B, L, _ = x.shape
    kv_first = (L - keep) // tl
    kern = functools.partial(_inproj_kernel, bb=bb, tl=tl, kv_first=kv_first)
    rows = lambda w: pl.BlockSpec((bb, tl, w), lambda b, i: (b, i, 0))
    return pl.pallas_call(
        kern,
        grid=(B // bb, L // tl),
        in_specs=[
            rows(D_MODEL),
            _mod_spec(0, bb, row0, layer),
            _mod_spec(1, bb, row0, layer),
            _of_layer((1, D_MODEL), layer),
            _of_layer((D_MODEL, IN_COLS), layer),
        ],
        out_specs=[
            rows(3 * ATT_W), rows(D_MODEL), rows(POOL_W),
            pl.BlockSpec((tl, bb * SSM_W), lambda b, i: (i, b)),
            pl.BlockSpec((bb, tl, 2 * ATT_W), lambda b, i: (b, jnp.maximum(i - kv_first, 0), 0)),
        ],
        out_shape=[
            jax.ShapeDtypeStruct((B, L, 3 * ATT_W), BF16),
            jax.ShapeDtypeStruct((B, L, D_MODEL), BF16),
            jax.ShapeDtypeStruct((B, L, POOL_W), F32),
            jax.ShapeDtypeStruct((L, B * SSM_W), F32),
            jax.ShapeDtypeStruct((B, keep, 2 * ATT_W), F32),
        ],
        scratch_shapes=[pltpu.VMEM((bb * tl, D_MODEL), BF16)],
        compiler_params=_params(("arbitrary", "arbitrary")),
        name="in_proj",
    )(x, mod, mod, g_all, w_all)


def _softmax_pv(scores, values):
    m = scores[0].max(axis=-1, keepdims=True)
    for sc in scores[1:]:
        m = jnp.maximum(m, sc.max(axis=-1, keepdims=True))
    den = None
    acc = None
    for sc, (val, transposed) in zip(scores, values):
        e = jnp.exp2(sc - m)
        d = jnp.sum(e, axis=-1, keepdims=True)
        dims = (((1,), (1,)), ((), ())) if transposed else (((1,), (0,)), ((), ()))
        o = lax.dot_general(e.astype(BF16), val, dims, preferred_element_type=F32)
        den = d if den is None else den + d
        acc = o if acc is None else acc + o
    return acc / den


_NT = (((1,), (1,)), ((), ()))


def _attn_kernel(q_ref, kp_ref, kc_ref, vp_ref, vc_ref, bias_ref, o_ref, b2, *, tq, sq, pairs):
    i = pl.program_id(2)
    lane = lax.broadcasted_iota(jnp.int32, (1, LANE), 1)
    own = [lane < HEAD_DIM, lane >= HEAD_DIM]
    hpp = LANE // HEAD_DIM
    n_sub = tq // sq

    def run(first_block):
        def pieces_of(s):
            pieces = [(kc_ref, vc_ref, 0, (s + 1) * sq, PREV_ROWS - s * sq)]
            if not first_block:
                pieces.insert(0, (kp_ref, vp_ref, s * sq, PREV_ROWS, 0))
            return pieces

        def scores_of(s, p, hh):
            ls = slice(p * LANE, (p + 1) * LANE)
            q = q_ref[0, s * sq:(s + 1) * sq, ls]
            qh = jnp.where(own[hh], q, jnp.zeros_like(q))
            return [lax.dot_general(qh, kr[0, a:b, ls], _NT, preferred_element_type=F32)
                    + b2[p * hpp + hh, :, c0:c0 + (b - a)] for kr, _, a, b, c0 in pieces_of(s)]

        def weighted_values(s, p, hh, scores):
            ls = slice(p * LANE, (p + 1) * LANE)
            m = scores[0].max(axis=-1, keepdims=True)
            for sc in scores[1:]:
                m = jnp.maximum(m, sc.max(axis=-1, keepdims=True))
            o = None
            for sc, (_, vr, a, b, _) in zip(scores, pieces_of(s)):
                v = vr[0, a:b, ls]
                vh = jnp.where(own[hh], v, jnp.ones_like(v))
                pv = jnp.dot(jnp.exp2(sc - m).astype(BF16), vh, preferred_element_type=F32)
                o = pv if o is None else o + pv
            return o

        chains = [(s, p, hh) for s in range(n_sub) for p in range(pairs) for hh in range(hpp)]
        scores, acc = {chains[0]: scores_of(*chains[0])}, {}
        for n, c in enumerate(chains):
            if n + 1 < len(chains):
                scores[chains[n + 1]] = scores_of(*chains[n + 1])
            acc[c] = weighted_values(*c, scores.pop(c))
        for s in range(n_sub):
            for p in range(pairs):
                a0, a1 = acc[(s, p, 0)], acc[(s, p, 1)]
                num = jnp.where(own[0], a0, a1)
                den = pltpu.roll(jnp.where(own[0], a1, a0), HEAD_DIM, 1)
                o_ref[0, s * sq:(s + 1) * sq, p * LANE:(p + 1) * LANE] = (num / den).astype(BF16)

    @pl.when(i == 0)
    def _():
        b2[...] = bias_ref[...] * LOG2E
        run(True)

    @pl.when(i > 0)
    def _():
        run(False)


def _attn_call(qkv, bias, layer, *, tq, sq, pairs):
    B, L, _ = qkv.shape
    assert tq == PREV_ROWS
    width = pairs * LANE
    groups = ATT_W // width
    heads = width // HEAD_DIM
    kern = functools.partial(_attn_kernel, tq=tq, sq=sq, pairs=pairs)
    cur = lambda c: pl.BlockSpec((1, tq, width), lambda b, h, i: (b, i, c * groups + h))
    prev = lambda c: pl.BlockSpec((1, tq, width), lambda b, h, i: (b, jnp.maximum(i - 1, 0), c * groups + h))
    return pl.pallas_call(
        kern,
        grid=(B, groups, L // tq),
        in_specs=[
            cur(0), prev(1), cur(1), prev(2), cur(2),
            pl.BlockSpec((None, heads, sq, PREV_ROWS + sq), lambda b, h, i: (layer, h, 0, 0)),
        ],
        out_specs=pl.BlockSpec((1, tq, width), lambda b, h, i: (b, i, h)),
        out_shape=jax.ShapeDtypeStruct((B, L, ATT_W), BF16),
        scratch_shapes=[pltpu.VMEM((heads, sq, PREV_ROWS + sq), F32)],
        compiler_params=_params(("arbitrary", "arbitrary", "arbitrary")),
        name="band_attn",
    )(qkv, qkv, qkv, qkv, qkv, bias)


def _attn_cached_kernel(qkv_ref, kt_ref, vt_ref, bias_ref, o_ref, *, tq):
    def scores_of(h):
        hs = slice(h * HEAD_DIM, (h + 1) * HEAD_DIM)
        q = qkv_ref[0, :, hs]
        k_new = qkv_ref[0, :, ATT_W + h * HEAD_DIM:ATT_W + (h + 1) * HEAD_DIM]
        kt = kt_ref[0, 0, h].astype(BF16)
        s_old = jnp.dot(q, kt, preferred_element_type=F32) + bias_ref[h, :, 0:PREV_ROWS] * LOG2E
        s_new = (lax.dot_general(q, k_new, _NT, preferred_element_type=F32)
                 + bias_ref[h, :, PREV_ROWS:] * LOG2E)
        return [s_old, s_new]

    def weighted_values(h, scores):
        v_new = qkv_ref[0, :, 2 * ATT_W + h * HEAD_DIM:2 * ATT_W + (h + 1) * HEAD_DIM]
        vt = vt_ref[0, 0, h].astype(BF16)
        return _softmax_pv(scores, [(vt, True), (v_new, False)])

    ahead = 4
    scores = {h: scores_of(h) for h in range(ahead)}
    outs = []
    for h in range(N_HEADS):
        if h + ahead < N_HEADS:
            scores[h + ahead] = scores_of(h + ahead)
        outs.append(weighted_values(h, scores.pop(h)))
    o_ref[0] = jnp.concatenate(outs, axis=-1).astype(BF16)


def _attn_cached_call(qkv, kt_cache, vt_cache, layer, bias, *, tq):
    B, L, _ = qkv.shape
    assert L == tq == CHUNK
    kern = functools.partial(_attn_cached_kernel, tq=tq)
    cache_spec = pl.BlockSpec((1, 1, N_HEADS, HEAD_DIM, PREV_ROWS), lambda b: (layer, b, 0, 0, 0))
    return pl.pallas_call(
        kern,
        grid=(B,),
        in_specs=[
            pl.BlockSpec((1, tq, 3 * ATT_W), lambda b: (b, 0, 0)),
            cache_spec, cache_spec,
            _of_layer((N_HEADS, tq, PREV_ROWS + tq), layer),
        ],
        out_specs=pl.BlockSpec((1, tq, ATT_W), lambda b: (b, 0, 0)),
        out_shape=jax.ShapeDtypeStruct((B, L, ATT_W), BF16),
        compiler_params=_params(("arbitrary",)),
        name="cached_attn",
    )(qkv, kt_cache, vt_cache, bias)


def _bias_kernel(t_ref, o_ref, *, sq, wl):
    width = t_ref.shape[-1]
    rows = pltpu.roll(jnp.broadcast_to(t_ref[0], (sq, width)), 0, 1, stride=1, stride_axis=0)
    i = lax.broadcasted_iota(jnp.int32, (sq, 1), 0)
    r = lax.broadcasted_iota(jnp.int32, (1, wl), 1)
    lo = jnp.bitwise_and(i, -CHUNK)
    band = (r >= lo) & (r < lo + BAND_CHUNKS * CHUNK)
    o_ref[0] = jnp.where(band, rows[:, :wl], NEG_INF)


def _attn_bias(rel_bias, sq):
    depth, H, _ = rel_bias.shape
    wl = PREV_ROWS + sq
    width = 1024
    assert width >= wl + sq - 1 and width % LANE == 0
    m = jnp.arange(width)
    m = jnp.where(m < wl, m, m - width)
    table = rel_bias[:, :, jnp.clip(PREV_ROWS - m, -REL_CLIP, REL_CLIP) + REL_CLIP]
    out = pl.pallas_call(
        functools.partial(_bias_kernel, sq=sq, wl=wl),
        grid=(depth * H,),
        in_specs=[pl.BlockSpec((1, 1, width), lambda n: (n, 0, 0))],
        out_specs=pl.BlockSpec((1, sq, wl), lambda n: (n, 0, 0)),
        out_shape=jax.ShapeDtypeStruct((depth * H, sq, wl), F32),
        compiler_params=_params(("arbitrary",)),
        name="attn_bias",
    )(table.reshape(depth * H, 1, width))
    return out.reshape(depth, H, sq, wl)


def _ssm_kernel(u_ref, h0r_ref, h0i_ref, are_ref, aim_ref, ldt_ref, bre_ref, bim_ref, cre_ref, cim_ref,
                d_ref, wglu_ref, bglu_ref, y_ref, htr_ref, hti_ref,
                a_scr, bbar_scr, cmat_scr, h_scr, carry_scr, *, B, Tb, nsub, cw):
    step = pl.program_id(0)
    rows = Tb * B
    ts = SSM_TILE_STATES

    @pl.when(step == 0)
    def _():
        dt = jnp.exp(ldt_ref[...])
        ar, ai = are_ref[...], aim_ref[...]
        mag = jnp.exp(ar * dt)
        ang = ai * dt
        abr, abi = mag * jnp.cos(ang), mag * jnp.sin(ang)
        den = ar * ar + ai * ai
        nr, ni = abr - 1.0, abi
        cr = (nr * ar + ni * ai) / den
        ci = (ni * ar - nr * ai) / den
        for j in range(SSM_LANE_TILES):
            sl = slice(ts * j, ts * (j + 1))
            a_scr[j, :, 0:ts] = jnp.broadcast_to(abr[:, sl], (SUBLANE, ts))
            a_scr[j, :, ts:2 * ts] = jnp.broadcast_to(abi[:, sl], (SUBLANE, ts))
            br, bi = bre_ref[j], bim_ref[j]
            bbar_scr[j, :, 0:ts] = (cr[:, sl] * br - ci[:, sl] * bi).astype(BF16)
            bbar_scr[j, :, ts:2 * ts] = (cr[:, sl] * bi + ci[:, sl] * br).astype(BF16)
            cmat_scr[j, 0:ts, :] = cre_ref[j].astype(BF16)
            cmat_scr[j, ts:2 * ts, :] = (-cim_ref[j]).astype(BF16)
            carry_scr[j, :, 0:ts] = h0r_ref[:, sl]
            carry_scr[j, :, ts:2 * ts] = h0i_ref[:, sl]

    tsub = Tb // nsub
    srows = tsub * B
    u = u_ref[...].reshape(rows, SSM_W)
    ub = u.astype(BF16)
    for sb in range(nsub):
        for j in range(SSM_LANE_TILES):
            h_scr[sb, j] = jnp.dot(ub[sb * srows:(sb + 1) * srows, LANE * j:LANE * (j + 1)], bbar_scr[j],
                                   preferred_element_type=F32)
    for sb in range(nsub):
        ys = []
        for j in range(SSM_LANE_TILES):
            for q in range(ts // cw):
                re_sl = slice(q * cw, (q + 1) * cw)
                im_sl = slice(ts + q * cw, ts + (q + 1) * cw)
                ar = jnp.broadcast_to(a_scr[j, 0:1, re_sl], (B, cw))
                ai = jnp.broadcast_to(a_scr[j, 0:1, im_sl], (B, cw))
                hr, hi = carry_scr[j, :, re_sl], carry_scr[j, :, im_sl]
                for t in range(tsub):
                    rs = slice(t * B, (t + 1) * B)
                    hr, hi = (ar * hr - ai * hi + h_scr[sb, j, rs, re_sl],
                              ar * hi + ai * hr + h_scr[sb, j, rs, im_sl])
                    h_scr[sb, j, rs, re_sl] = hr
                    h_scr[sb, j, rs, im_sl] = hi
                carry_scr[j, :, re_sl] = hr
                carry_scr[j, :, im_sl] = hi
            ys.append(jnp.dot(h_scr[sb, j].astype(BF16), cmat_scr[j], preferred_element_type=F32))
        y = jnp.concatenate(ys, axis=-1) + d_ref[...] * u[sb * srows:(sb + 1) * srows]
        g = jax.nn.gelu(y)
        gl = jnp.dot(g.astype(BF16), wglu_ref[...], preferred_element_type=F32) + bglu_ref[...]
        out = gl[:, :SSM_W] * jax.nn.sigmoid(gl[:, SSM_W:])
        y_ref[sb * tsub:(sb + 1) * tsub] = out.reshape(tsub, B, SSM_W)

    @pl.when(step == pl.num_programs(0) - 1)
    def _():
        for j in range(SSM_LANE_TILES):
            sl = slice(ts * j, ts * (j + 1))
            htr_ref[:, sl] = carry_scr[j, :, 0:ts]
            hti_ref[:, sl] = carry_scr[j, :, ts:2 * ts]


def _ssm_call(u_tm, h0r, h0i, h0_layer, sp, wglu_all, bglu_all, layer, *, Tb, nsub):
    L, B, _ = u_tm.shape
    rows = Tb * B
    cw = max(LANE, min(SSM_TILE_STATES, 4096 // B))
    kern = functools.partial(_ssm_kernel, B=B, Tb=Tb, nsub=nsub, cw=cw)
    full = lambda shape: pl.BlockSpec(shape, lambda i: (0,) * len(shape))
    nt, ts = SSM_LANE_TILES, SSM_TILE_STATES
    return pl.pallas_call(
        kern,
        grid=(L // Tb,),
        in_specs=[
            pl.BlockSpec((Tb, B, SSM_W), lambda i: (i, 0, 0)),
            _of_layer((B, N_STATES), h0_layer), _of_layer((B, N_STATES), h0_layer),
            _of_layer((1, N_STATES), layer), _of_layer((1, N_STATES), layer), _of_layer((1, N_STATES), layer),
            _of_layer((nt, LANE, ts), layer), _of_layer((nt, LANE, ts), layer),
            _of_layer((nt, ts, LANE), layer), _of_layer((nt, ts, LANE), layer),
            _of_layer((1, SSM_W), layer), _of_layer((SSM_W, 2 * SSM_W), layer),
            _of_layer((1, 2 * SSM_W), layer),
        ],
        out_specs=[
            pl.BlockSpec((Tb, B, SSM_W), lambda i: (i, 0, 0)),
            full((B, N_STATES)), full((B, N_STATES)),
        ],
        out_shape=[
            jax.ShapeDtypeStruct((L, B, SSM_W), F32),
            jax.ShapeDtypeStruct((B, N_STATES), F32),
            jax.ShapeDtypeStruct((B, N_STATES), F32),
        ],
        scratch_shapes=[
            pltpu.VMEM((nt, SUBLANE, 2 * ts), F32),
            pltpu.VMEM((nt, LANE, 2 * ts), BF16),
            pltpu.VMEM((nt, 2 * ts, LANE), BF16),
            pltpu.VMEM((nsub, nt, rows // nsub, 2 * ts), F32),
            pltpu.VMEM((nt, B, 2 * ts), F32),
        ],
        compiler_params=_params(("arbitrary",)),
        name="s5_scan",
    )(u_tm, h0r, h0i, sp["a_re"], sp["a_im"], sp["log_dt"], sp["b_re"], sp["b_im"],
      sp["c_re"], sp["c_im"], sp["d"], wglu_all, bglu_all)


def _ssm_layouts(a_re, a_im, log_dt, b_re, b_im, c_re, c_im, d_skip):
    depth = a_re.shape[0]
    gpt = SSM_GROUPS // SSM_LANE_TILES
    eye = jnp.eye(gpt, dtype=bool)[None, None, :, None, :, None]

    def blockdiag(m, rows, cols):
        return jnp.where(eye, m[:, :, :, :, None, :], 0.0).reshape(depth, SSM_LANE_TILES, rows, cols)

    def b_tiles(b):
        bt = jnp.transpose(b, (0, 1, 3, 2)).reshape(depth, SSM_LANE_TILES, gpt, SSM_GC, SSM_STATE)
        return blockdiag(bt, LANE, SSM_TILE_STATES)

    def c_tiles(c):
        ct = jnp.transpose(c, (0, 1, 3, 2)).reshape(depth, SSM_LANE_TILES, gpt, SSM_STATE, SSM_GC)
        return blockdiag(ct, SSM_TILE_STATES, LANE)

    return {
        "a_re": a_re.reshape(depth, 1, N_STATES), "a_im": a_im.reshape(depth, 1, N_STATES),
        "log_dt": jnp.repeat(log_dt, SSM_STATE, axis=1).reshape(depth, 1, N_STATES),
        "b_re": b_tiles(b_re), "b_im": b_tiles(b_im),
        "c_re": c_tiles(c_re), "c_im": c_tiles(c_im),
        "d": d_skip.reshape(depth, 1, SSM_W),
    }


def _pool_mix(u, halo, first_pos, w_ref, scale):
    tl = u.shape[0]
    ext = jnp.concatenate([halo, u], axis=0)
    sums = []
    s = ext
    for k in (1, 2, 4, 8):
        s = s + pltpu.roll(s, k, 0)
        sums.append(s)
    pos = first_pos + lax.broadcasted_iota(jnp.int32, (tl, 1), 0)
    outs = []
    for g, w in enumerate(POOL_WINDOWS):
        ls = slice(g * POOL_GROUP_W, (g + 1) * POOL_GROUP_W)
        cnt = jnp.minimum(w, pos + 1).astype(F32)
        diff = sums[g][POOL_HALO:, ls] / cnt - u[:, ls]
        outs.append(jnp.dot(diff.astype(BF16), w_ref[g], preferred_element_type=F32))
    return jnp.concatenate(outs, axis=-1) * scale


def _gated_norm(y, z, g):
    y = y.astype(F32)
    ms = jnp.mean(y * y, axis=-1, keepdims=True)
    return (y * lax.rsqrt(ms + EPS) * g * jax.nn.silu(z.astype(F32))).astype(BF16)


def _outproj_kernel(x_ref, ya_ref, ys_ref, up_ref, halo_ref, hist_ref, wp_ref, ps_ref, z_ref, g_ref, gate_ref,
                    w_ref, fg_ref, o_ref, *, bb, tl, final, start_pos):
    i = pl.program_id(1)
    tm = bb * tl
    g = g_ref[...]
    z = z_ref[...].reshape(tm, D_MODEL)
    ya = ya_ref[...].reshape(tm, ATT_W)
    ys_tm = ys_ref[...]
    ys = jnp.concatenate([ys_tm[:, b * SSM_W:(b + 1) * SSM_W] for b in range(bb)], axis=0)
    yp = jnp.concatenate(
        [_pool_mix(up_ref[b], jnp.where(i == 0, hist_ref[b], halo_ref[b]), start_pos + i * tl, wp_ref, ps_ref[...])
         for b in range(bb)], axis=0)
    a_att = _gated_norm(ya, z[:, 0:ATT_W], g[:, 0:ATT_W])
    a_ssm = _gated_norm(ys, z[:, ATT_W:ATT_W + SSM_W], g[:, ATT_W:ATT_W + SSM_W])
    a_pool = _gated_norm(yp, z[:, ATT_W + SSM_W:], g[:, ATT_W + SSM_W:])
    acc = jnp.dot(a_att, w_ref[0:ATT_W, :], preferred_element_type=F32)
    acc = acc + jnp.dot(a_ssm, w_ref[ATT_W:ATT_W + SSM_W, :], preferred_element_type=F32)
    acc = acc + jnp.dot(a_pool, w_ref[ATT_W + SSM_W:, :], preferred_element_type=F32)
    x_new = x_ref[...] + gate_ref[...] * acc.reshape(bb, tl, D_MODEL)
    if final:
        ms = jnp.mean(x_new * x_new, axis=-1, keepdims=True)
        x_new = x_new * lax.rsqrt(ms + EPS) * fg_ref[...]
    o_ref[...] = x_new


def _outproj_call(x, y_att, y_ssm_tm, u_pool, pool_hist, hist_layer, wp_all, ps_all, z, g_all, mod, row0, w_all,
                  final_g, layer, *, bb, tl, final, start_pos):
    B, L, _ = x.shape
    kern = functools.partial(_outproj_kernel, bb=bb, tl=tl, final=final, start_pos=start_pos)
    rows = lambda w: pl.BlockSpec((bb, tl, w), lambda b, i: (b, i, 0))
    hpb = tl // POOL_HALO
    return pl.pallas_call(
        kern,
        grid=(B // bb, L // tl),
        in_specs=[
            rows(D_MODEL), rows(ATT_W),
            pl.BlockSpec((tl, bb * SSM_W), lambda b, i: (i, b)),
            rows(POOL_W),
            pl.BlockSpec((bb, POOL_HALO, POOL_W), lambda b, i: (b, jnp.maximum(i * hpb - 1, 0), 0)),
            pl.BlockSpec((None, bb, POOL_HALO, POOL_W), lambda b, i: (hist_layer, b, 0, 0)),
            _of_layer((len(POOL_WINDOWS), POOL_GROUP_W, POOL_GROUP_W), layer),
            _of_layer((1, POOL_W), layer),
            rows(D_MODEL),
            _of_layer((1, D_MODEL), layer),
            _mod_spec(2, bb, row0, layer),
            _of_layer((D_MODEL, D_MODEL), layer),
            _resident((1, D_MODEL)),
        ],
        out_specs=rows(D_MODEL),
        out_shape=jax.ShapeDtypeStruct((B, L, D_MODEL), F32),
        compiler_params=_params(("arbitrary", "arbitrary")),
        name="out_proj",
    )(x, y_att, y_ssm_tm, u_pool, u_pool, pool_hist, wp_all, ps_all, z, g_all, mod, w_all, final_g)


def _layer(x, layer, row0, attend, h0, pool_hist, pw, *, start_pos, keep, cfg, final):
    B, L, _ = x.shape
    qkv, z, u_pool, u_tm, kv = _inproj_call(x, pw["mod"], row0, pw["norm_g"], pw["w_in"], layer,
                                            bb=cfg["in_bb"], tl=cfg["in_tl"], keep=keep)
    y_att = attend(qkv)
    y_ssm_tm, htr, hti = _ssm_call(u_tm.reshape(L, B, SSM_W), *h0, pw["ssm"], pw["w_glu"], pw["b_glu"], layer,
                                   Tb=cfg["ssm_tb"], nsub=cfg["ssm_nsub"])
    x_new = _outproj_call(x, y_att, y_ssm_tm.reshape(L, B * SSM_W), u_pool, *pool_hist, pw["w_pool"],
                          pw["pool_scale"], z, pw["branch_g"], pw["mod"], row0, pw["w_out"], pw["final_g"], layer,
                          bb=cfg["out_bb"], tl=cfg["out_tl"], final=final, start_pos=start_pos)
    return x_new, kv, u_pool, htr, hti


def kernel(x_prompt, x_sample, c_prompt, c_sample, cache_k, cache_v, state_ssm_re, state_ssm_im, state_pool, norm_g, w_ada, b_ada, w_in, rel_bias, ssm_a_re, ssm_a_im, ssm_log_dt, ssm_b_re, ssm_b_im, ssm_c_re, ssm_c_im, ssm_d, w_glu, b_glu, w_pool, pool_scale, branch_norm_g, w_out, final_norm_g):
    depth = w_in.shape[0]
    Bp, Lp, _ = x_prompt.shape
    Bs, Ls, _ = x_sample.shape
    hist_s = cache_k.shape[2]
    assert Lp % PREV_ROWS == 0 and Ls == CHUNK and hist_s == PREV_ROWS and Ls >= POOL_HIST
    keep = min(PREV_ROWS, Lp)

    cfg_p = dict(in_bb=1, in_tl=256, tq=PREV_ROWS, sq=256, pairs=4, ssm_tb=min(Lp, 1024 // Bp), ssm_nsub=2,
                 out_bb=1, out_tl=512)
    cfg_s = dict(in_bb=256 // Ls, in_tl=Ls, tq=Ls, ssm_tb=min(Ls, 1024 // Bs), ssm_nsub=2,
                 out_bb=256 // Ls, out_tl=Ls)

    n_rows = Bp + Bs
    mod = _ada_call(jnp.concatenate([c_prompt, c_sample], axis=0), w_ada, b_ada)
    bias_p = _attn_bias(rel_bias, cfg_p["sq"])
    bias_s = _attn_bias(rel_bias, Ls)
    kt_cache = jnp.transpose(cache_k, (0, 1, 3, 4, 2))
    vt_cache = jnp.transpose(cache_v, (0, 1, 3, 4, 2))
    pw = {
        "mod": jnp.transpose(mod.reshape(depth, n_rows, 3, D_MODEL), (0, 2, 1, 3))[:, :, :, None, :],
        "norm_g": norm_g.reshape(depth, 1, D_MODEL),
        "w_in": w_in.astype(BF16),
        "ssm": _ssm_layouts(ssm_a_re, ssm_a_im, ssm_log_dt, ssm_b_re, ssm_b_im, ssm_c_re, ssm_c_im, ssm_d),
        "w_glu": w_glu.astype(BF16),
        "b_glu": b_glu.reshape(depth, 1, 2 * SSM_W),
        "w_pool": w_pool.astype(BF16),
        "pool_scale": pool_scale.reshape(depth, 1, POOL_W),
        "branch_g": branch_norm_g.reshape(depth, 1, D_MODEL),
        "w_out": w_out.astype(BF16),
        "final_g": final_norm_g.reshape(1, D_MODEL),
    }
    zeros_state = jnp.zeros((1, Bp, N_STATES), F32)
    zeros_pool = jnp.zeros((1, Bp, POOL_HALO, POOL_W), F32)
    state_re = state_ssm_re.reshape(depth, Bs, N_STATES)
    state_im = state_ssm_im.reshape(depth, Bs, N_STATES)
    pool_hist_s = jnp.pad(state_pool, ((0, 0), (0, 0), (POOL_HALO - POOL_HIST, 0), (0, 0)))

    xp, xs = x_prompt, x_sample
    outs = [[] for _ in range(10)]
    for l in range(depth):
        final = l == depth - 1
        attend_p = functools.partial(_attn_call, bias=bias_p, layer=l, tq=cfg_p["tq"], sq=cfg_p["sq"],
                                     pairs=cfg_p["pairs"])
        xp, kv_p, up_p, htr_p, hti_p = _layer(xp, l, 0, attend_p, (zeros_state, zeros_state, 0), (zeros_pool, 0), pw,
                                              start_pos=0, keep=keep, cfg=cfg_p, final=final)
        attend_s = functools.partial(_attn_cached_call, kt_cache=kt_cache, vt_cache=vt_cache, layer=l,
                                     bias=bias_s, tq=cfg_s["tq"])
        xs, kv_s, up_s, htr_s, hti_s = _layer(xs, l, Bp, attend_s, (state_re, state_im, l), (pool_hist_s, l), pw,
                                              start_pos=PAST_LEN, keep=Ls, cfg=cfg_s, final=final)

        heads = lambda a: a.reshape(a.shape[0], a.shape[1], N_HEADS, HEAD_DIM)
        state = lambda a: a.reshape(a.shape[0], SSM_GROUPS, SSM_STATE)
        outs[0].append(heads(kv_p[:, :, :ATT_W]))
        outs[1].append(heads(kv_p[:, :, ATT_W:]))
        outs[2].append(state(htr_p))
        outs[3].append(state(hti_p))
        outs[4].append(up_p[:, Lp - POOL_HIST:])
        outs[5].append(heads(kv_s[:, :, :ATT_W]))
        outs[6].append(heads(kv_s[:, :, ATT_W:]))
        outs[7].append(state(htr_s))
        outs[8].append(state(hti_s))
        outs[9].append(up_s[:, Ls - POOL_HIST:])

    return (xp, xs) + tuple(jnp.stack(o) for o in outs)
```
